```python
import jax
import jax.numpy as jnp
from jax import lax
import numpy as np

D_MODEL = 1024
BATCH = 4
SEQ = 4096
DEPTH = 2

MEM_LEN = 256
W_A = D_MODEL
A_CONV = 3
W_B = D_MODEL
B_CONV = 31
X_HEADS = 4
X_HEAD_DIM = D_MODEL // X_HEADS
W_X = X_HEADS * X_HEAD_DIM
N_BRANCH = 3
N_IN = 3 * W_A + 2 * W_B + W_X + N_BRANCH * D_MODEL
SPLITS = [W_A, 2 * W_A, 3 * W_A, 3 * W_A + 2 * W_B, 3 * W_A + 2 * W_B + W_X]
N_GROUPS = 4
EXPERTS_PER_GROUP = 8
N_EXPERTS = N_GROUPS * EXPERTS_PER_GROUP
TOP_K_IN_GROUP = 2
D_FF_EXPERT = D_MODEL // 2
EPS = 1e-6

kernel_name = "hybrid_gated_conv_xattn_hmoe"


def rms_norm(x, g):
    xf = x.astype(jnp.float32)
    y = xf * lax.rsqrt(jnp.mean(xf * xf, axis=-1, keepdims=True) + EPS)
    return (y * g.astype(jnp.float32)).astype(x.dtype)


def layer_norm(x, g, b):
    xf = x.astype(jnp.float32)
    mu = jnp.mean(xf, axis=-1, keepdims=True)
    xc = xf - mu
    var = jnp.mean(xc * xc, axis=-1, keepdims=True)
    y = xc * lax.rsqrt(var + EPS) * g.astype(jnp.float32) + b.astype(jnp.float32)
    return y.astype(x.dtype)


def causal_depthwise_conv(u, w):
    k, c = w.shape
    return lax.conv_general_dilated(
        u, w[:, None, :].astype(u.dtype), window_strides=(1,), padding=[(k - 1, 0)],
        dimension_numbers=("NWC", "WIO", "NWC"), feature_group_count=c)


def setup_inputs(seed: int = 0) -> dict:
    key = jax.random.key(seed)
    ks = jax.random.split(key, 26)
    f32 = jnp.float32
    nrm = lambda k, shape, scale: jax.random.normal(k, shape, f32) * scale
    L, D = DEPTH, D_MODEL
    return {
        "x": nrm(ks[0], (BATCH, SEQ, D), 1.0),
        "mem": nrm(ks[1], (BATCH, MEM_LEN, D), 1.0),
        "mix_norm_g": 1.0 + nrm(ks[2], (L, D), 0.02),
        "mem_norm_g": 1.0 + nrm(ks[3], (L, D), 0.02),
        "w_in": nrm(ks[4], (L, D, N_IN), D ** -0.5),
        "b_gate": nrm(ks[5], (L, N_BRANCH * D), 0.02),
        "conv_a_w": nrm(ks[6], (L, A_CONV, W_A), A_CONV ** -0.5),
        "w_out_a": nrm(ks[7], (L, W_A, D), W_A ** -0.5),
        "b_glu": nrm(ks[8], (L, 2 * W_B), 0.02),
        "conv_b_w": nrm(ks[9], (L, B_CONV, W_B), B_CONV ** -0.5),
        "conv_b_b": nrm(ks[10], (L, W_B), 0.02),
        "ln_b_g": 1.0 + nrm(ks[11], (L, W_B), 0.02),
        "ln_b_b": nrm(ks[12], (L, W_B), 0.02),
        "w_out_b": nrm(ks[13], (L, W_B, D), W_B ** -0.5),
        "b_out_b": nrm(ks[14], (L, D), 0.02),
        "w_kv": nrm(ks[15], (L, D, 2 * W_X), D ** -0.5),
        "w_out_x": nrm(ks[16], (L, W_X, D), W_X ** -0.5),
        "w_o": nrm(ks[17], (L, D, D), D ** -0.5),
        "ffn_norm_g": 1.0 + nrm(ks[18], (L, D), 0.02),
        "w_group": nrm(ks[19], (L, D, N_GROUPS), D ** -0.5),
        "b_group": nrm(ks[20], (L, N_GROUPS), 0.01),
        "w_router": nrm(ks[21], (L, D, N_EXPERTS), D ** -0.5),
        "b_router": nrm(ks[22], (L, N_EXPERTS), 0.01),
        "w_gate_up": nrm(ks[23], (L, N_EXPERTS, D, 2 * D_FF_EXPERT), D ** -0.5),
        "w_down": nrm(ks[24], (L, N_EXPERTS, D_FF_EXPERT, D), D_FF_EXPERT ** -0.5),
        "final_norm_g": 1.0 + nrm(ks[25], (D,), 0.02),
    }


def cross_attention(q, mem_n, w_kv):
    b, s, _ = q.shape
    kv = mem_n @ w_kv
    k, v = jnp.split(kv, 2, axis=-1)
    qh = q.reshape(b, s, X_HEADS, X_HEAD_DIM)
    kh = k.reshape(b, -1, X_HEADS, X_HEAD_DIM)
    vh = v.reshape(b, -1, X_HEADS, X_HEAD_DIM)
    scores = jnp.einsum("bshd,bmhd->bhsm", qh, kh).astype(jnp.float32) * (X_HEAD_DIM ** -0.5)
    probs = jax.nn.softmax(scores, axis=-1).astype(vh.dtype)
    out = jnp.einsum("bhsm,bmhd->bshd", probs, vh)
    return out.reshape(b, s, W_X)


def hierarchical_moe(z, w_group, b_group, w_router, b_router, w_gate_up, w_down):
    g_logits = (z @ w_group).astype(jnp.float32) + b_group.astype(jnp.float32)
    g_probs = jax.nn.softmax(g_logits, axis=-1)
    g_sel = jnp.argmax(g_probs, axis=-1)
    g_p = jnp.max(g_probs, axis=-1)
    e_logits = (z @ w_router).astype(jnp.float32) + b_router.astype(jnp.float32)
    e_logits = e_logits.reshape(z.shape[0], z.shape[1], N_GROUPS, EXPERTS_PER_GROUP)
    e_in_group = jnp.take_along_axis(e_logits, g_sel[..., None, None], axis=2)[..., 0, :]
    e_probs = jax.nn.softmax(e_in_group, axis=-1)
    top_p, top_i = lax.top_k(e_probs, TOP_K_IN_GROUP)
    top_w = top_p / jnp.sum(top_p, axis=-1, keepdims=True)
    global_idx = g_sel[..., None] * EXPERTS_PER_GROUP + top_i
    combine = jnp.sum(jax.nn.one_hot(global_idx, N_EXPERTS, dtype=jnp.float32) * top_w[..., None], axis=-2)
    combine = (combine * g_p[..., None]).astype(z.dtype)
    y = jnp.zeros_like(z)
    for e in range(N_EXPERTS):
        gate, up = jnp.split(z @ w_gate_up[e], 2, axis=-1)
        y = y + combine[..., e:e + 1] * ((jax.nn.silu(gate) * up) @ w_down[e])
    return y


def reference(x, mem, mix_norm_g, mem_norm_g, w_in, b_gate, conv_a_w, w_out_a, b_glu,
              conv_b_w, conv_b_b, ln_b_g, ln_b_b, w_out_b, b_out_b, w_kv, w_out_x, w_o,
              ffn_norm_g, w_group, b_group, w_router, b_router, w_gate_up, w_down,
              final_norm_g):
    b, s, d = x.shape
    for l in range(DEPTH):
        x_n = rms_norm(x, mix_norm_g[l])
        h = x_n @ w_in[l]
        a_val, a_gb, a_gc, h_glu, q, h_gate = jnp.split(h, SPLITS, axis=-1)

        y_a = (a_gb * causal_depthwise_conv(a_gc * a_val, conv_a_w[l])) @ w_out_a[l]

        glu_val, glu_gate = jnp.split(h_glu + b_glu[l], 2, axis=-1)
        u = glu_val * jax.nn.sigmoid(glu_gate)
        u = causal_depthwise_conv(u, conv_b_w[l]) + conv_b_b[l]
        u = jax.nn.silu(layer_norm(u, ln_b_g[l], ln_b_b[l]))
        y_b = u @ w_out_b[l] + b_out_b[l]

        mem_n = rms_norm(mem, mem_norm_g[l])
        y_x = cross_attention(q, mem_n, w_kv[l]) @ w_out_x[l]

        gates = jax.nn.sigmoid(h_gate + b_gate[l]).reshape(b, s, N_BRANCH, d)
        merged = gates[..., 0, :] * y_a + gates[..., 1, :] * y_b + gates[..., 2, :] * y_x
        x = x + merged @ w_o[l]

        z = rms_norm(x, ffn_norm_g[l])
        x = x + hierarchical_moe(z, w_group[l], b_group[l], w_router[l], b_router[l],
                                 w_gate_up[l], w_down[l])
    return rms_norm(x, final_norm_g)
```

```python
import functools

import jax
import jax.numpy as jnp
from jax import lax
from jax.experimental import pallas as pl
from jax.experimental.pallas import tpu as pltpu

F32 = jnp.float32
BF16 = jnp.bfloat16

D_MODEL = 1024
N_HEADS = 4
HEAD_DIM = D_MODEL // N_HEADS
MEM_LEN = 256
A_TAPS = 3
B_TAPS = 31
N_GROUPS = 4
GROUP_SIZE = 8
N_EXPERTS = N_GROUPS * GROUP_SIZE
D_FF = D_MODEL // 2
N_SEG = 9
EPS = 1e-6

LANES = 128
A_HIST = 8
B_HIST = 32

PROJ_TM = 2048
PROJ_TN = 1024
MIX_T = 256
SUBLANES = 8
CONV_ROWS = 32
CONV_LANES = 256
ROUTE_T = 512
DISPATCH_T = 512
EXPERT_TM = 256
COMBINE_T = 256
ROUTER_COL0 = N_GROUPS

VMEM_LIMIT = 56 * 1024 * 1024


def _rms(x, g):
    return x * lax.rsqrt(jnp.mean(x * x, axis=-1, keepdims=True) + EPS) * g


def _sigmoid(x):
    return 1.0 / (1.0 + jnp.exp(-x))


def _kv_kernel(mem_ref, g_ref, w_ref, k_ref, v_ref):
    mn = _rms(mem_ref[0], g_ref[...]).astype(BF16)
    kv = jnp.dot(mn, w_ref[...], preferred_element_type=F32)
    k_ref[0] = (kv[:, :D_MODEL] * (HEAD_DIM ** -0.5)).astype(BF16)
    v_ref[0] = kv[:, D_MODEL:].astype(BF16)


def _kv_proj(mem, g, w_kv):
    b = mem.shape[0]
    return pl.pallas_call(
        _kv_kernel,
        grid=(b,),
        in_specs=[
            pl.BlockSpec((1, MEM_LEN, D_MODEL), lambda i: (i, 0, 0)),
            pl.BlockSpec((1, D_MODEL), lambda i: (0, 0)),
            pl.BlockSpec((D_MODEL, 2 * D_MODEL), lambda i: (0, 0)),
        ],
        out_specs=[
            pl.BlockSpec((1, MEM_LEN, D_MODEL), lambda i: (i, 0, 0)),
            pl.BlockSpec((1, MEM_LEN, D_MODEL), lambda i: (i, 0, 0)),
        ],
        out_shape=[jax.ShapeDtypeStruct((b, MEM_LEN, D_MODEL), BF16)] * 2,
        compiler_params=pltpu.CompilerParams(vmem_limit_bytes=VMEM_LIMIT),
        name="kv_proj",
    )(mem, g, w_kv)


def _proj_in_kernel(x_ref, g_ref, w_ref, o_ref, xn_ref):
    @pl.when(pl.program_id(1) == 0)
    def _():
        xn_ref[...] = _rms(x_ref[...], g_ref[...]).astype(BF16)

    o_ref[...] = jnp.dot(xn_ref[...], w_ref[...], preferred_element_type=F32).astype(BF16)


def _proj_in(x, g, w):
    n = x.shape[0]
    n_in = w.shape[1]
    return pl.pallas_call(
        _proj_in_kernel,
        grid=(n // PROJ_TM, n_in // PROJ_TN),
        in_specs=[
            pl.BlockSpec((PROJ_TM, D_MODEL), lambda i, j: (i, 0)),
            pl.BlockSpec((1, D_MODEL), lambda i, j: (0, 0)),
            pl.BlockSpec((D_MODEL, PROJ_TN), lambda i, j: (0, j)),
        ],
        out_specs=pl.BlockSpec((PROJ_TM, PROJ_TN), lambda i, j: (i, j)),
        out_shape=jax.ShapeDtypeStruct((n, n_in), BF16),
        scratch_shapes=[pltpu.VMEM((PROJ_TM, D_MODEL), BF16)],
        compiler_params=pltpu.CompilerParams(
            dimension_semantics=("arbitrary", "arbitrary"), vmem_limit_bytes=VMEM_LIMIT),
        name="proj_in",
    )(x, g, w)


def _mixer_kernel(aval, agb, agc, gluv, glug, q_ref, g0, g1, g2, x_ref, k_ref, v_ref,
                  caw, cbw, bglu, cbb, lng, lnb, bob, bgate, woa, wob, wox, wo,
                  o_ref, cvbuf, ubuf, nbuf):
    t = MIX_T

    @pl.when(pl.program_id(1) == 0)
    def _():
        cvbuf[0:A_HIST, :] = jnp.zeros((A_HIST, D_MODEL), F32)
        ubuf[0:B_HIST, :] = jnp.zeros((B_HIST, D_MODEL), F32)

    cvbuf[A_HIST:A_HIST + t, :] = agc[...].astype(F32) * aval[...].astype(F32)
    conv_a = caw[0:1, :] * cvbuf[A_HIST - 2:A_HIST - 2 + t, :]
    conv_a += caw[1:2, :] * cvbuf[A_HIST - 1:A_HIST - 1 + t, :]
    conv_a += caw[2:3, :] * cvbuf[A_HIST:A_HIST + t, :]
    ya_in = (agb[...].astype(F32) * conv_a).astype(BF16)
    y_a = jnp.dot(ya_in, woa[...], preferred_element_type=F32)
    merged = _sigmoid(g0[...].astype(F32) + bgate[:, 0:D_MODEL]) * y_a
    cvbuf[0:A_HIST, :] = cvbuf[t:t + A_HIST, :]

    gv = gluv[...].astype(F32) + bglu[:, 0:D_MODEL]
    gg = glug[...].astype(F32) + bglu[:, D_MODEL:2 * D_MODEL]
    ubuf[B_HIST:B_HIST + t, :] = gv * _sigmoid(gg)

    def conv_chunk(c, carry):
        base = pl.multiple_of(c * CONV_ROWS, CONV_ROWS)
        win_rows = CONV_ROWS + B_HIST
        parts = []
        for lc in range(D_MODEL // CONV_LANES):
            ls = slice(lc * CONV_LANES, (lc + 1) * CONV_LANES)
            win = ubuf[pl.ds(base, win_rows), ls]
            part = jnp.zeros((CONV_ROWS, CONV_LANES), F32) + cbb[:, ls]
            for res in range(SUBLANES):
                rolled = win if res == 0 else pltpu.roll(win, win_rows - res, 0)
                for tap in range(B_TAPS):
                    off = B_HIST - (B_TAPS - 1) + tap
                    if off % SUBLANES == res:
                        al = off - res
                        part += cbw[tap:tap + 1, ls] * rolled[al:al + CONV_ROWS, :]
            parts.append(part)
        acc = jnp.concatenate(parts, axis=-1)
        mu = jnp.mean(acc, axis=-1, keepdims=True)
        xc = acc - mu
        var = jnp.mean(xc * xc, axis=-1, keepdims=True)
        ln = xc * lax.rsqrt(var + EPS) * lng[...] + lnb[...]
        nbuf[pl.ds(base, CONV_ROWS), :] = (ln * _sigmoid(ln)).astype(BF16)
        return carry

    lax.fori_loop(0, t // CONV_ROWS, conv_chunk, 0)
    ubuf[0:B_HIST, :] = ubuf[t:t + B_HIST, :]
    y_b = jnp.dot(nbuf[...], wob[...], preferred_element_type=F32) + bob[...]
    merged += _sigmoid(g1[...].astype(F32) + bgate[:, D_MODEL:2 * D_MODEL]) * y_b

    heads = []
    for h in range(N_HEADS):
        sl = slice(h * HEAD_DIM, (h + 1) * HEAD_DIM)
        s = lax.dot_general(q_ref[:, sl], k_ref[0, :, sl], (((1,), (1,)), ((), ())),
                            preferred_element_type=F32)
        p = jnp.exp(s - jnp.max(s, axis=-1, keepdims=True))
        denom = jnp.sum(p, axis=-1, keepdims=True)
        o = jnp.dot(p.astype(BF16), v_ref[0, :, sl], preferred_element_type=F32)
        heads.append((o / denom).astype(BF16))
    attn = jnp.concatenate(heads, axis=-1)
    y_x = jnp.dot(attn, wox[...], preferred_element_type=F32)
    merged += _sigmoid(g2[...].astype(F32) + bgate[:, 2 * D_MODEL:3 * D_MODEL]) * y_x

    o_ref[...] = x_ref[...] + jnp.dot(merged.astype(BF16), wo[...], preferred_element_type=F32)


def _mixer(h, x, k, v, caw, cbw, bglu, cbb, lng, lnb, bob, bgate, woa, wob, wox, wo, batch):
    n = x.shape[0]
    n_s = n // batch // MIX_T
    row = lambda b, s: b * n_s + s
    seg_specs = [pl.BlockSpec((MIX_T, D_MODEL), functools.partial(lambda b, s, c: (row(b, s), c), c=c))
                 for c in range(N_SEG)]
    const2 = lambda shape: pl.BlockSpec(shape, lambda b, s: (0, 0))
    kv_spec = pl.BlockSpec((1, MEM_LEN, D_MODEL), lambda b, s: (b, 0, 0))
    w_spec = const2((D_MODEL, D_MODEL))
    return pl.pallas_call(
        _mixer_kernel,
        grid=(batch, n_s),
        in_specs=seg_specs + [
            pl.BlockSpec((MIX_T, D_MODEL), lambda b, s: (row(b, s), 0)),
            kv_spec, kv_spec,
            const2((A_TAPS, D_MODEL)), const2((B_TAPS, D_MODEL)),
            const2((1, 2 * D_MODEL)), const2((1, D_MODEL)), const2((1, D_MODEL)),
            const2((1, D_MODEL)), const2((1, D_MODEL)), const2((1, 3 * D_MODEL)),
            w_spec, w_spec, w_spec, w_spec,
        ],
        out_specs=pl.BlockSpec((MIX_T, D_MODEL), lambda b, s: (row(b, s), 0)),
        out_shape=jax.ShapeDtypeStruct((n, D_MODEL), F32),
        scratch_shapes=[
            pltpu.VMEM((A_HIST + MIX_T, D_MODEL), F32),
            pltpu.VMEM((B_HIST + MIX_T, D_MODEL), F32),
            pltpu.VMEM((MIX_T, D_MODEL), BF16),
        ],
        compiler_params=pltpu.CompilerParams(
            dimension_semantics=("arbitrary", "arbitrary"), vmem_limit_bytes=VMEM_LIMIT),
        name="mixer",
    )(*([h] * N_SEG), x, k, v, caw, cbw, bglu, cbb, lng, lnb, bob, bgate, woa, wob, wox, wo)


def _route_kernel(x_ref, g_ref, w_ref, b_ref, z_ref, meta_ref, cnt_ref, cnt_acc):
    t = ROUTE_T

    @pl.when(pl.program_id(0) == 0)
    def _():
        cnt_acc[...] = jnp.zeros_like(cnt_acc)

    z = _rms(x_ref[...], g_ref[...])
    z_ref[...] = z
    logits = jnp.dot(z, w_ref[...], preferred_element_type=F32,
                     precision=lax.Precision.HIGHEST) + b_ref[...]
    lane = lax.broadcasted_iota(jnp.int32, (t, LANES), 1).astype(F32)
    neg = jnp.float32(-jnp.inf)
    big = jnp.float32(1e9)

    gmask = lane < N_GROUPS
    gl = jnp.where(gmask, logits, neg)
    gmax = jnp.max(gl, axis=-1, keepdims=True)
    g_sel = jnp.min(jnp.where(gl == gmax, lane, big), axis=-1, keepdims=True)
    g_p = 1.0 / jnp.sum(jnp.where(gmask, jnp.exp(logits - gmax), 0.0), axis=-1, keepdims=True)

    lo = ROUTER_COL0 + GROUP_SIZE * g_sel
    el = jnp.where((lane >= lo) & (lane < lo + GROUP_SIZE), logits, neg)
    m1 = jnp.max(el, axis=-1, keepdims=True)
    i1 = jnp.min(jnp.where(el == m1, lane, big), axis=-1, keepdims=True)
    el2 = jnp.where(lane == i1, neg, el)
    m2 = jnp.max(el2, axis=-1, keepdims=True)
    i2 = jnp.min(jnp.where(el2 == m2, lane, big), axis=-1, keepdims=True)
    r = jnp.exp(m2 - m1)
    c1 = g_p / (1.0 + r)
    c2 = g_p * r / (1.0 + r)

    oh1 = lane == i1
    oh2 = lane == i2
    ri = lax.broadcasted_iota(jnp.int32, (t, t), 0)
    ci = lax.broadcasted_iota(jnp.int32, (t, t), 1)
    tri = jnp.where(ci < ri, 1.0, 0.0).astype(BF16)
    oh1f = jnp.where(oh1, 1.0, 0.0)
    oh2f = jnp.where(oh2, 1.0, 0.0)
    pre1 = jnp.dot(tri, oh1f.astype(BF16), preferred_element_type=F32)
    pre2 = jnp.dot(tri, oh2f.astype(BF16), preferred_element_type=F32)
    tot1 = jnp.sum(oh1f, axis=0, keepdims=True)
    tot2 = jnp.sum(oh2f, axis=0, keepdims=True)
    cnt = cnt_acc[...]
    rank1 = jnp.sum(oh1f * (pre1 + cnt), axis=-1, keepdims=True)
    rank2 = jnp.sum(oh2f * (pre2 + tot1 + cnt), axis=-1, keepdims=True)
    cnt_new = cnt + tot1 + tot2
    cnt_acc[...] = cnt_new
    cnt_ref[...] = cnt_new

    meta = jnp.where(lane == 0, i1 - ROUTER_COL0, 0.0)
    meta = jnp.where(lane == 1, i2 - ROUTER_COL0, meta)
    meta = jnp.where(lane == 2, rank1, meta)
    meta = jnp.where(lane == 3, rank2, meta)
    meta = jnp.where(lane == 4, c1, meta)
    meta = jnp.where(lane == 5, c2, meta)
    meta_ref[...] = meta


def _route(x, g, w, b):
    n = x.shape[0]
    return pl.pallas_call(
        _route_kernel,
        grid=(n // ROUTE_T,),
        in_specs=[
            pl.BlockSpec((ROUTE_T, D_MODEL), lambda i: (i, 0)),
            pl.BlockSpec((1, D_MODEL), lambda i: (0, 0)),
            pl.BlockSpec((D_MODEL, LANES), lambda i: (0, 0)),
            pl.BlockSpec((1, LANES), lambda i: (0, 0)),
        ],
        out_specs=[
            pl.BlockSpec((ROUTE_T, D_MODEL), lambda i: (i, 0)),
            pl.BlockSpec((ROUTE_T, LANES), lambda i: (i, 0)),
            pl.BlockSpec((1, LANES), lambda i: (0, 0)),
        ],
        out_shape=[
            jax.ShapeDtypeStruct((n, D_MODEL), F32),
            jax.ShapeDtypeStruct((n, LANES), F32),
            jax.ShapeDtypeStruct((1, LANES), F32),
        ],
        scratch_shapes=[pltpu.VMEM((1, LANES), F32)],
        compiler_params=pltpu.CompilerParams(
            dimension_semantics=("arbitrary",), vmem_limit_bytes=VMEM_LIMIT),
        name="route",
    )(x, g, w, b)


def _dispatch_kernel(pos_ref, z_ref, init_ref, out_ref, sem):
    del init_ref

    def row_copy(r, k):
        return pltpu.make_async_copy(
            z_ref.at[pl.ds(r, 1), :], out_ref.at[pl.ds(pos_ref[2 * r + k], 1), :], sem)

    def start(r, carry):
        row_copy(r, 0).start()
        row_copy(r, 1).start()
        return carry

    def wait(r, carry):
        row_copy(r, 0).wait()
        row_copy(r, 1).wait()
        return carry

    lax.fori_loop(0, DISPATCH_T, start, 0)
    lax.fori_loop(0, DISPATCH_T, wait, 0)


def _dispatch(pos_flat, z, n_slots):
    n = z.shape[0]
    init = jnp.zeros((n_slots, D_MODEL), F32)
    return pl.pallas_call(
        _dispatch_kernel,
        grid=(n // DISPATCH_T,),
        in_specs=[
            pl.BlockSpec((2 * DISPATCH_T,), lambda i: (i,), memory_space=pltpu.SMEM),
            pl.BlockSpec((DISPATCH_T, D_MODEL), lambda i: (i, 0)),
            pl.BlockSpec(memory_space=pl.ANY),
        ],
        out_specs=pl.BlockSpec(memory_space=pl.ANY),
        out_shape=jax.ShapeDtypeStruct((n_slots, D_MODEL), F32),
        scratch_shapes=[pltpu.SemaphoreType.DMA],
        input_output_aliases={2: 0},
        compiler_params=pltpu.CompilerParams(
            dimension_semantics=("arbitrary",), vmem_limit_bytes=VMEM_LIMIT),
        name="dispatch",
    )(pos_flat, z, init)


def _experts_kernel(te_ref, nu_ref, z_ref, wgu_ref, wd_ref, y_ref):
    del te_ref
    used = pl.program_id(0) < nu_ref[0]

    @pl.when(used)
    def _():
        gu = jnp.dot(z_ref[...].astype(BF16), wgu_ref[0], preferred_element_type=F32)
        gate = gu[:, :D_FF]
        up = gu[:, D_FF:]
        hmid = (gate * _sigmoid(gate) * up).astype(BF16)
        y_ref[...] = jnp.dot(hmid, wd_ref[0], preferred_element_type=F32)

    @pl.when(jnp.logical_not(used))
    def _():
        y_ref[...] = jnp.zeros_like(y_ref)


def _experts(tile_expert, n_used, zs, wgu, wd):
    n_slots = zs.shape[0]
    n_tiles = n_slots // EXPERT_TM
    last = lambda j, nu: jnp.minimum(j, nu[0] - 1)
    return pl.pallas_call(
        _experts_kernel,
        grid_spec=pltpu.PrefetchScalarGridSpec(
            num_scalar_prefetch=2,
            grid=(n_tiles,),
            in_specs=[
                pl.BlockSpec((EXPERT_TM, D_MODEL), lambda j, te, nu: (last(j, nu), 0)),
                pl.BlockSpec((1, D_MODEL, 2 * D_FF), lambda j, te, nu: (te[j], 0, 0)),
                pl.BlockSpec((1, D_FF, D_MODEL), lambda j, te, nu: (te[j], 0, 0)),
            ],
            out_specs=pl.BlockSpec((EXPERT_TM, D_MODEL), lambda j, te, nu: (j, 0)),
        ),
        out_shape=jax.ShapeDtypeStruct((n_slots, D_MODEL), F32),
        compiler_params=pltpu.CompilerParams(
            dimension_semantics=("arbitrary",), vmem_limit_bytes=VMEM_LIMIT),
        name="experts",
    )(tile_expert, n_used, zs, wgu, wd)


def _combine_kernel(pos_ref, x_ref, meta_ref, g_ref, y_ref, o_ref, buf0, buf1, sem, *, final_norm):
    bufs = (buf0, buf1)

    def row_copy(r, k):
        return pltpu.make_async_copy(
            y_ref.at[pl.ds(pos_ref[2 * r + k], 1), :], bufs[k].at[pl.ds(r, 1), :], sem)

    def start(r, carry):
        row_copy(r, 0).start()
        row_copy(r, 1).start()
        return carry

    def wait(r, carry):
        row_copy(r, 0).wait()
        row_copy(r, 1).wait()
        return carry

    lax.fori_loop(0, COMBINE_T, start, 0)
    lax.fori_loop(0, COMBINE_T, wait, 0)
    meta = meta_ref[...]
    out = x_ref[...] + meta[:, 4:5] * buf0[...] + meta[:, 5:6] * buf1[...]
    if final_norm:
        out = _rms(out, g_ref[...])
    o_ref[...] = out


def _combine(pos_flat, x, meta, g, y, final_norm):
    n = x.shape[0]
    return pl.pallas_call(
        functools.partial(_combine_kernel, final_norm=final_norm),
        grid=(n // COMBINE_T,),
        in_specs=[
            pl.BlockSpec((2 * COMBINE_T,), lambda i: (i,), memory_space=pltpu.SMEM),
            pl.BlockSpec((COMBINE_T, D_MODEL), lambda i: (i, 0)),
            pl.BlockSpec((COMBINE_T, LANES), lambda i: (i, 0)),
            pl.BlockSpec((1, D_MODEL), lambda i: (0, 0)),
            pl.BlockSpec(memory_space=pl.ANY),
        ],
        out_specs=pl.BlockSpec((COMBINE_T, D_MODEL), lambda i: (i, 0)),
        out_shape=jax.ShapeDtypeStruct((n, D_MODEL), F32),
        scratch_shapes=[
            pltpu.VMEM((COMBINE_T, D_MODEL), F32),
            pltpu.VMEM((COMBINE_T, D_MODEL), F32),
            pltpu.SemaphoreType.DMA,
        ],
        compiler_params=pltpu.CompilerParams(
            dimension_semantics=("arbitrary",), vmem_limit_bytes=VMEM_LIMIT),
        name="combine",
    )(pos_flat, x, meta, g, y)


def _slot_plan(meta, counts, n_tiles):
    cnt = counts[0, ROUTER_COL0:ROUTER_COL0 + N_EXPERTS].astype(jnp.int32)
    padded = (cnt + EXPERT_TM - 1) // EXPERT_TM * EXPERT_TM
    ends = jnp.cumsum(padded)
    offs = ends - padded
    e = meta[:, 0:2].astype(jnp.int32)
    rank = meta[:, 2:4].astype(jnp.int32)
    onehot = e[..., None] == jnp.arange(N_EXPERTS, dtype=jnp.int32)
    pos = jnp.sum(jnp.where(onehot, offs, 0), axis=-1) + rank
    tile_ends = ends // EXPERT_TM
    tile_ids = jnp.arange(n_tiles, dtype=jnp.int32)
    tile_expert = jnp.sum(tile_ids[:, None] >= tile_ends[None, :], axis=1).astype(jnp.int32)
    tile_expert = jnp.minimum(tile_expert, N_EXPERTS - 1)
    n_used = tile_ends[-1:].astype(jnp.int32)
    return pos.reshape(-1), tile_expert, n_used


def kernel(x, mem, mix_norm_g, mem_norm_g, w_in, b_gate, conv_a_w, w_out_a, b_glu, conv_b_w,
           conv_b_b, ln_b_g, ln_b_b, w_out_b, b_out_b, w_kv, w_out_x, w_o, ffn_norm_g, w_group,
           b_group, w_router, b_router, w_gate_up, w_down, final_norm_g):
    batch, seq, d = x.shape
    depth = w_in.shape[0]
    n = batch * seq
    n_tiles = (2 * n) // EXPERT_TM + N_EXPERTS
    n_slots = n_tiles * EXPERT_TM
    row = lambda a: a.reshape(1, -1)
    xf = x.reshape(n, d)
    pad = LANES - N_GROUPS - N_EXPERTS
    for l in range(depth):
        k, v = _kv_proj(mem, row(mem_norm_g[l]), w_kv[l].astype(BF16))
        h = _proj_in(xf, row(mix_norm_g[l]), w_in[l].astype(BF16))
        x1 = _mixer(h, xf, k, v, conv_a_w[l], conv_b_w[l], row(b_glu[l]), row(conv_b_b[l]),
                    row(ln_b_g[l]), row(ln_b_b[l]), row(b_out_b[l]), row(b_gate[l]),
                    w_out_a[l].astype(BF16), w_out_b[l].astype(BF16), w_out_x[l].astype(BF16),
                    w_o[l].astype(BF16), batch)
        w_r = jnp.pad(jnp.concatenate([w_group[l], w_router[l]], axis=1), ((0, 0), (0, pad)))
        b_r = jnp.pad(jnp.concatenate([b_group[l], b_router[l]]), (0, pad)).reshape(1, LANES)
        z, meta, counts = _route(x1, row(ffn_norm_g[l]), w_r, b_r)
        pos, tile_expert, n_used = _slot_plan(meta, counts, n_tiles)
        zs = _dispatch(pos, z, n_slots)
        y = _experts(tile_expert, n_used, zs, w_gate_up[l].astype(BF16), w_down[l].astype(BF16))
        xf = _combine(pos, x1, meta, row(final_norm_g), y, final_norm=(l == depth - 1))
    return xf.reshape(batch, seq, d)
```

```python
import functools

import jax
import jax.numpy as jnp
from jax import lax
from jax.experimental import pallas as pl
from jax.experimental.pallas import tpu as pltpu

F32 = jnp.float32
BF16 = jnp.bfloat16

D_MODEL = 1024
N_HEADS = 4
HEAD_DIM = D_MODEL // N_HEADS
MEM_LEN = 256
A_TAPS = 3
B_TAPS = 31
N_GROUPS = 4
GROUP_SIZE = 8
N_EXPERTS = N_GROUPS * GROUP_SIZE
D_FF = D_MODEL // 2
N_SEG = 9
EPS = 1e-6

LANES = 128
SUBLANES = 8
N_COL = D_MODEL // LANES

PROJ_TM = 2048
PROJ_TN = 1024
MIX_T = 256
A_HIST = 8
B_HIST = 32
A_M = (MIX_T + A_HIST) // SUBLANES
B_M = (MIX_T + B_HIST) // SUBLANES
ROUTE_T = 512
DISPATCH_T = 512
EXPERT_TM = 256
COMBINE_T = 256
ROW_UNROLL = 8
ROUTER_COL0 = N_GROUPS

VMEM_LIMIT = 56 * 1024 * 1024


def _rms(x, g):
    return x * lax.rsqrt(jnp.mean(x * x, axis=-1, keepdims=True) + EPS) * g


def _sigmoid(x):
    return 1.0 / (1.0 + jnp.exp(-x))


def _cols(x):
    return [x[:, c * LANES:(c + 1) * LANES] for c in range(N_COL)]


def _token_tile(ref, r):
    return ref.at[pl.ds(pl.multiple_of(r * N_COL, N_COL), N_COL), :]


def _load_tiled(ref, rows):
    return jnp.concatenate([ref[pl.ds(c, rows, stride=N_COL), :] for c in range(N_COL)], axis=-1)


def _store_tiled(ref, x):
    for c, part in enumerate(_cols(x)):
        ref[pl.ds(c, x.shape[0], stride=N_COL), :] = part


def _kv_kernel(mem_ref, g_ref, w_ref, k_ref, v_ref):
    mn = _rms(mem_ref[0], g_ref[...]).astype(BF16)
    kv = jnp.dot(mn, w_ref[...], preferred_element_type=F32)
    k_ref[0] = (kv[:, :D_MODEL] * (HEAD_DIM ** -0.5)).astype(BF16)
    v_ref[0] = kv[:, D_MODEL:].astype(BF16)


def _kv_proj(mem, g, w_kv):
    b = mem.shape[0]
    return pl.pallas_call(
        _kv_kernel,
        grid=(b,),
        in_specs=[
            pl.BlockSpec((1, MEM_LEN, D_MODEL), lambda i: (i, 0, 0)),
            pl.BlockSpec((1, D_MODEL), lambda i: (0, 0)),
            pl.BlockSpec((D_MODEL, 2 * D_MODEL), lambda i: (0, 0)),
        ],
        out_specs=[
            pl.BlockSpec((1, MEM_LEN, D_MODEL), lambda i: (i, 0, 0)),
            pl.BlockSpec((1, MEM_LEN, D_MODEL), lambda i: (i, 0, 0)),
        ],
        out_shape=[jax.ShapeDtypeStruct((b, MEM_LEN, D_MODEL), BF16)] * 2,
        compiler_params=pltpu.CompilerParams(vmem_limit_bytes=VMEM_LIMIT),
        name="kv_proj",
    )(mem, g, w_kv)


def _proj_in_kernel(x_ref, g_ref, w_ref, o_ref, xn_ref):
    @pl.when(pl.program_id(1) == 0)
    def _():
        xn_ref[...] = _rms(x_ref[...], g_ref[...]).astype(BF16)

    o_ref[...] = jnp.dot(xn_ref[...], w_ref[0].astype(BF16),
                         preferred_element_type=F32).astype(BF16)


def _proj_in(x, g, w_all, layer):
    n = x.shape[0]
    n_in = w_all.shape[2]
    return pl.pallas_call(
        _proj_in_kernel,
        grid=(n // PROJ_TM, n_in // PROJ_TN),
        in_specs=[
            pl.BlockSpec((PROJ_TM, D_MODEL), lambda i, j: (i, 0)),
            pl.BlockSpec((1, D_MODEL), lambda i, j: (0, 0)),
            pl.BlockSpec((1, D_MODEL, PROJ_TN), lambda i, j: (layer, 0, j)),
        ],
        out_specs=pl.BlockSpec((PROJ_TM, PROJ_TN), lambda i, j: (i, j)),
        out_shape=jax.ShapeDtypeStruct((n, n_in), BF16),
        scratch_shapes=[pltpu.VMEM((PROJ_TM, D_MODEL), BF16)],
        compiler_params=pltpu.CompilerParams(
            dimension_semantics=("arbitrary", "arbitrary"), vmem_limit_bytes=VMEM_LIMIT),
        name="proj_in",
    )(x, g, w_all)


def _causal_conv(src, wrap, dst, taps, bias, col, m, n_taps):
    first_wrap = m - (n_taps - 1)
    view = lambda ref, j: ref[col, pl.ds(j, SUBLANES, stride=m), :]
    for j in range(first_wrap, m):
        wrap[col, pl.ds((j - first_wrap) * SUBLANES, SUBLANES), :] = pltpu.roll(view(src, j), 1, 0)
    w = [jnp.broadcast_to(taps[col, k:k + 1, :], (SUBLANES, LANES)) for k in range(n_taps)]
    for i in range(m):
        acc = None if bias is None else jnp.broadcast_to(bias[col], (SUBLANES, LANES))
        for s in range(n_taps):
            j = i - s
            if j >= 0:
                x = view(src, j)
            else:
                x = wrap[col, pl.ds((j + m - first_wrap) * SUBLANES, SUBLANES), :]
            term = w[n_taps - 1 - s] * x
            acc = term if acc is None else acc + term
        dst[col, pl.ds(i, SUBLANES, stride=m), :] = acc


def _mixer_kernel(aval, agb, agc, gluv, glug, q_ref, g0, g1, g2, x_ref, k_ref, v_ref,
                  caw, cbw, bglu, cbb, lng, lnb, bob, bgate, woa, wob, wox, wo,
                  o_ref, a_src, a_wrap, a_dst, b_src, b_wrap, b_dst):
    t = MIX_T

    @pl.when(pl.program_id(1) == 0)
    def _():
        a_src[:, 0:A_HIST, :] = jnp.zeros((N_COL, A_HIST, LANES), F32)
        b_src[:, 0:B_HIST, :] = jnp.zeros((N_COL, B_HIST, LANES), F32)

    cv = agc[...].astype(F32) * aval[...].astype(F32)
    for c, part in enumerate(_cols(cv)):
        a_src[c, A_HIST:A_HIST + t, :] = part
    for c in range(N_COL):
        _causal_conv(a_src, a_wrap, a_dst, caw, None, c, A_M, A_TAPS)
    conv_a = jnp.concatenate([a_dst[c, A_HIST:A_HIST + t, :] for c in range(N_COL)], axis=-1)
    a_src[:, 0:A_HIST, :] = a_src[:, t:t + A_HIST, :]
    ya_in = (agb[...].astype(F32) * conv_a).astype(BF16)
    y_a = jnp.dot(ya_in, woa[...], preferred_element_type=F32)
    merged = _sigmoid(g0[...].astype(F32) + bgate[:, 0:D_MODEL]) * y_a

    gv = gluv[...].astype(F32) + bglu[:, 0:D_MODEL]
    gg = glug[...].astype(F32) + bglu[:, D_MODEL:2 * D_MODEL]
    for c, part in enumerate(_cols(gv * _sigmoid(gg))):
        b_src[c, B_HIST:B_HIST + t, :] = part

    def conv_col(c, carry):
        _causal_conv(b_src, b_wrap, b_dst, cbw, cbb, c, B_M, B_TAPS)
        return carry

    lax.fori_loop(0, N_COL, conv_col, 0)
    conv_b = jnp.concatenate([b_dst[c, B_HIST:B_HIST + t, :] for c in range(N_COL)], axis=-1)
    b_src[:, 0:B_HIST, :] = b_src[:, t:t + B_HIST, :]
    mu = jnp.mean(conv_b, axis=-1, keepdims=True)
    xc = conv_b - mu
    var = jnp.mean(xc * xc, axis=-1, keepdims=True)
    ln = xc * lax.rsqrt(var + EPS) * lng[...] + lnb[...]
    y_b = jnp.dot((ln * _sigmoid(ln)).astype(BF16), wob[...], preferred_element_type=F32) + bob[...]
    merged += _sigmoid(g1[...].astype(F32) + bgate[:, D_MODEL:2 * D_MODEL]) * y_b

    heads = []
    for h in range(N_HEADS):
        sl = slice(h * HEAD_DIM, (h + 1) * HEAD_DIM)
        s = lax.dot_general(q_ref[:, sl], k_ref[0, :, sl], (((1,), (1,)), ((), ())),
                            preferred_element_type=F32)
        p = jnp.exp(s - jnp.max(s, axis=-1, keepdims=True))
        denom = jnp.sum(p, axis=-1, keepdims=True)
        o = jnp.dot(p.astype(BF16), v_ref[0, :, sl], preferred_element_type=F32)
        heads.append((o / denom).astype(BF16))
    attn = jnp.concatenate(heads, axis=-1)
    y_x = jnp.dot(attn, wox[...], preferred_element_type=F32)
    merged += _sigmoid(g2[...].astype(F32) + bgate[:, 2 * D_MODEL:3 * D_MODEL]) * y_x

    o_ref[...] = x_ref[...] + jnp.dot(merged.astype(BF16), wo[...], preferred_element_type=F32)


def _mixer(h, x, k, v, caw, cbw, bglu, cbb, lng, lnb, bob, bgate, woa, wob, wox, wo, batch):
    n = x.shape[0]
    n_s = n // batch // MIX_T
    row = lambda b, s: b * n_s + s
    seg_specs = [pl.BlockSpec((MIX_T, D_MODEL), functools.partial(lambda b, s, c: (row(b, s), c), c=c))
                 for c in range(N_SEG)]
    const2 = lambda shape: pl.BlockSpec(shape, lambda b, s: (0, 0))
    const3 = lambda shape: pl.BlockSpec(shape, lambda b, s: (0, 0, 0))
    kv_spec = pl.BlockSpec((1, MEM_LEN, D_MODEL), lambda b, s: (b, 0, 0))
    w_spec = const2((D_MODEL, D_MODEL))
    col_buf = lambda rows: pltpu.VMEM((N_COL, rows, LANES), F32)
    return pl.pallas_call(
        _mixer_kernel,
        grid=(batch, n_s),
        in_specs=seg_specs + [
            pl.BlockSpec((MIX_T, D_MODEL), lambda b, s: (row(b, s), 0)),
            kv_spec, kv_spec,
            const3((N_COL, A_TAPS, LANES)), const3((N_COL, B_TAPS, LANES)),
            const2((1, 2 * D_MODEL)), const3((N_COL, 1, LANES)), const2((1, D_MODEL)),
            const2((1, D_MODEL)), const2((1, D_MODEL)), const2((1, 3 * D_MODEL)),
            w_spec, w_spec, w_spec, w_spec,
        ],
        out_specs=pl.BlockSpec((MIX_T, D_MODEL), lambda b, s: (row(b, s), 0)),
        out_shape=jax.ShapeDtypeStruct((n, D_MODEL), F32),
        scratch_shapes=[
            col_buf(SUBLANES * A_M), col_buf(SUBLANES * (A_TAPS - 1)), col_buf(SUBLANES * A_M),
            col_buf(SUBLANES * B_M), col_buf(SUBLANES * (B_TAPS - 1)), col_buf(SUBLANES * B_M),
        ],
        compiler_params=pltpu.CompilerParams(
            dimension_semantics=("arbitrary", "arbitrary"), vmem_limit_bytes=VMEM_LIMIT),
        name="mixer",
    )(*([h] * N_SEG), x, k, v, caw, cbw, bglu, cbb, lng, lnb, bob, bgate, woa, wob, wox, wo)


def _route_kernel(x_ref, g_ref, w_ref, b_ref, z_ref, meta_ref, cnt_ref, cnt_acc):
    t = ROUTE_T

    @pl.when(pl.program_id(0) == 0)
    def _():
        cnt_acc[...] = jnp.zeros_like(cnt_acc)

    z = _rms(x_ref[...], g_ref[...])
    _store_tiled(z_ref, z)
    logits = jnp.dot(z, w_ref[...], preferred_element_type=F32,
                     precision=lax.Precision.HIGHEST) + b_ref[...]
    lane = lax.broadcasted_iota(jnp.int32, (t, LANES), 1).astype(F32)
    neg = jnp.float32(-jnp.inf)
    big = jnp.float32(1e9)

    gmask = lane < N_GROUPS
    gl = jnp.where(gmask, logits, neg)
    gmax = jnp.max(gl, axis=-1, keepdims=True)
    g_sel = jnp.min(jnp.where(gl == gmax, lane, big), axis=-1, keepdims=True)
    g_p = 1.0 / jnp.sum(jnp.where(gmask, jnp.exp(logits - gmax), 0.0), axis=-1, keepdims=True)

    lo = ROUTER_COL0 + GROUP_SIZE * g_sel
    el = jnp.where((lane >= lo) & (lane < lo + GROUP_SIZE), logits, neg)
    m1 = jnp.max(el, axis=-1, keepdims=True)
    i1 = jnp.min(jnp.where(el == m1, lane, big), axis=-1, keepdims=True)
    el2 = jnp.where(lane == i1, neg, el)
    m2 = jnp.max(el2, axis=-1, keepdims=True)
    i2 = jnp.min(jnp.where(el2 == m2, lane, big), axis=-1, keepdims=True)
    r = jnp.exp(m2 - m1)
    c1 = g_p / (1.0 + r)
    c2 = g_p * r / (1.0 + r)

    oh1 = lane == i1
    oh2 = lane == i2
    ri = lax.broadcasted_iota(jnp.int32, (t, t), 0)
    ci = lax.broadcasted_iota(jnp.int32, (t, t), 1)
    tri = jnp.where(ci < ri, 1.0, 0.0).astype(BF16)
    oh1f = jnp.where(oh1, 1.0, 0.0)
    oh2f = jnp.where(oh2, 1.0, 0.0)
    pre1 = jnp.dot(tri, oh1f.astype(BF16), preferred_element_type=F32)
    pre2 = jnp.dot(tri, oh2f.astype(BF16), preferred_element_type=F32)
    tot1 = jnp.sum(oh1f, axis=0, keepdims=True)
    tot2 = jnp.sum(oh2f, axis=0, keepdims=True)
    cnt = cnt_acc[...]
    rank1 = jnp.sum(oh1f * (pre1 + cnt), axis=-1, keepdims=True)
    rank2 = jnp.sum(oh2f * (pre2 + tot1 + cnt), axis=-1, keepdims=True)
    cnt_new = cnt + tot1 + tot2
    cnt_acc[...] = cnt_new
    cnt_ref[...] = cnt_new

    meta = jnp.where(lane == 0, i1 - ROUTER_COL0, 0.0)
    meta = jnp.where(lane == 1, i2 - ROUTER_COL0, meta)
    meta = jnp.where(lane == 2, rank1, meta)
    meta = jnp.where(lane == 3, rank2, meta)
    meta = jnp.where(lane == 4, c1, meta)
    meta = jnp.where(lane == 5, c2, meta)
    meta_ref[...] = meta


def _route(x, g, w, b):
    n = x.shape[0]
    return pl.pallas_call(
        _route_kernel,
        grid=(n // ROUTE_T,),
        in_specs=[
            pl.BlockSpec((ROUTE_T, D_MODEL), lambda i: (i, 0)),
            pl.BlockSpec((1, D_MODEL), lambda i: (0, 0)),
            pl.BlockSpec((D_MODEL, LANES), lambda i: (0, 0)),
            pl.BlockSpec((1, LANES), lambda i: (0, 0)),
        ],
        out_specs=[
            pl.BlockSpec((ROUTE_T * N_COL, LANES), lambda i: (i, 0)),
            pl.BlockSpec((ROUTE_T, LANES), lambda i: (i, 0)),
            pl.BlockSpec((1, LANES), lambda i: (0, 0)),
        ],
        out_shape=[
            jax.ShapeDtypeStruct((n * N_COL, LANES), F32),
            jax.ShapeDtypeStruct((n, LANES), F32),
            jax.ShapeDtypeStruct((1, LANES), F32),
        ],
        scratch_shapes=[pltpu.VMEM((1, LANES), F32)],
        compiler_params=pltpu.CompilerParams(
            dimension_semantics=("arbitrary",), vmem_limit_bytes=VMEM_LIMIT),
        name="route",
    )(x, g, w, b)


def _dispatch_kernel(pos_ref, z_ref, init_ref, out_ref, sem):
    del init_ref

    def start(g, carry):
        for u in range(ROW_UNROLL):
            r = g * ROW_UNROLL + u
            for k in range(2):
                pltpu.make_async_copy(
                    _token_tile(z_ref, r), _token_tile(out_ref, pos_ref[2 * r + k]), sem).start()
        return carry

    lax.fori_loop(0, DISPATCH_T // ROW_UNROLL, start, 0)
    for _ in range(2):
        pltpu.make_async_copy(z_ref, out_ref.at[pl.ds(0, DISPATCH_T * N_COL), :], sem).wait()


def _dispatch(pos_flat, z, n_slots):
    n = z.shape[0] // N_COL
    init = jnp.zeros((n_slots * N_COL, LANES), F32)
    return pl.pallas_call(
        _dispatch_kernel,
        grid=(n // DISPATCH_T,),
        in_specs=[
            pl.BlockSpec((2 * DISPATCH_T,), lambda i: (i,), memory_space=pltpu.SMEM),
            pl.BlockSpec((DISPATCH_T * N_COL, LANES), lambda i: (i, 0)),
            pl.BlockSpec(memory_space=pl.ANY),
        ],
        out_specs=pl.BlockSpec(memory_space=pl.ANY),
        out_shape=jax.ShapeDtypeStruct((n_slots * N_COL, LANES), F32),
        scratch_shapes=[pltpu.SemaphoreType.DMA],
        input_output_aliases={2: 0},
        compiler_params=pltpu.CompilerParams(
            dimension_semantics=("arbitrary",), vmem_limit_bytes=VMEM_LIMIT),
        name="dispatch",
    )(pos_flat, z, init)


def _experts_kernel(te_ref, nu_ref, z_ref, wgu_ref, wd_ref, y_ref, wgu_bf, wd_bf):
    j = pl.program_id(0)
    used = j < nu_ref[0]
    new_expert = (j == 0) | (te_ref[j] != te_ref[jnp.maximum(j - 1, 0)])

    @pl.when(used & new_expert)
    def _():
        wgu_bf[...] = wgu_ref[0, 0].astype(BF16)
        wd_bf[...] = wd_ref[0, 0].astype(BF16)

    @pl.when(used)
    def _():
        z = _load_tiled(z_ref, EXPERT_TM).astype(BF16)
        gu = jnp.dot(z, wgu_bf[...], preferred_element_type=F32)
        gate = gu[:, :D_FF]
        up = gu[:, D_FF:]
        hmid = (gate * _sigmoid(gate) * up).astype(BF16)
        _store_tiled(y_ref, jnp.dot(hmid, wd_bf[...], preferred_element_type=F32))

    @pl.when(jnp.logical_not(used))
    def _():
        y_ref[...] = jnp.zeros_like(y_ref)


def _experts(tile_expert, n_used, zs, wgu_all, wd_all, layer):
    n_slots = zs.shape[0] // N_COL
    n_tiles = n_slots // EXPERT_TM
    last = lambda j, nu: jnp.minimum(j, nu[0] - 1)
    return pl.pallas_call(
        _experts_kernel,
        grid_spec=pltpu.PrefetchScalarGridSpec(
            num_scalar_prefetch=2,
            grid=(n_tiles,),
            in_specs=[
                pl.BlockSpec((EXPERT_TM * N_COL, LANES), lambda j, te, nu: (last(j, nu), 0)),
                pl.BlockSpec((1, 1, D_MODEL, 2 * D_FF), lambda j, te, nu: (layer, te[j], 0, 0)),
                pl.BlockSpec((1, 1, D_FF, D_MODEL), lambda j, te, nu: (layer, te[j], 0, 0)),
            ],
            out_specs=pl.BlockSpec((EXPERT_TM * N_COL, LANES), lambda j, te, nu: (j, 0)),
            scratch_shapes=[pltpu.VMEM((D_MODEL, 2 * D_FF), BF16), pltpu.VMEM((D_FF, D_MODEL), BF16)],
        ),
        out_shape=jax.ShapeDtypeStruct((n_slots * N_COL, LANES), F32),
        compiler_params=pltpu.CompilerParams(
            dimension_semantics=("arbitrary",), vmem_limit_bytes=VMEM_LIMIT),
        name="experts",
    )(tile_expert, n_used, zs, wgu_all, wd_all)


def _combine_kernel(pos_ref, x_ref, meta_ref, g_ref, y_ref, o_ref, buf0, buf1, sem, *, final_norm):
    bufs = (buf0, buf1)

    def start(g, carry):
        for u in range(ROW_UNROLL):
            r = g * ROW_UNROLL + u
            for k in range(2):
                pltpu.make_async_copy(
                    _token_tile(y_ref, pos_ref[2 * r + k]), _token_tile(bufs[k], r), sem).start()
        return carry

    lax.fori_loop(0, COMBINE_T // ROW_UNROLL, start, 0)
    for k in range(2):
        pltpu.make_async_copy(y_ref.at[pl.ds(0, COMBINE_T * N_COL), :], bufs[k], sem).wait()
    meta = meta_ref[...]
    y0 = _load_tiled(buf0, COMBINE_T)
    y1 = _load_tiled(buf1, COMBINE_T)
    out = x_ref[...] + meta[:, 4:5] * y0 + meta[:, 5:6] * y1
    if final_norm:
        out = _rms(out, g_ref[...])
    o_ref[...] = out


def _combine(pos_flat, x, meta, g, y, final_norm):
    n = x.shape[0]
    return pl.pallas_call(
        functools.partial(_combine_kernel, final_norm=final_norm),
        grid=(n // COMBINE_T,),
        in_specs=[
            pl.BlockSpec((2 * COMBINE_T,), lambda i: (i,), memory_space=pltpu.SMEM),
            pl.BlockSpec((COMBINE_T, D_MODEL), lambda i: (i, 0)),
            pl.BlockSpec((COMBINE_T, LANES), lambda i: (i, 0)),
            pl.BlockSpec((1, D_MODEL), lambda i: (0, 0)),
            pl.BlockSpec(memory_space=pl.ANY),
        ],
        out_specs=pl.BlockSpec((COMBINE_T, D_MODEL), lambda i: (i, 0)),
        out_shape=jax.ShapeDtypeStruct((n, D_MODEL), F32),
        scratch_shapes=[
            pltpu.VMEM((COMBINE_T * N_COL, LANES), F32),
            pltpu.VMEM((COMBINE_T * N_COL, LANES), F32),
            pltpu.SemaphoreType.DMA,
        ],
        compiler_params=pltpu.CompilerParams(
            dimension_semantics=("arbitrary",), vmem_limit_bytes=VMEM_LIMIT),
        name="combine",
    )(pos_flat, x, meta, g, y)


def _slot_plan(meta, counts, n_tiles):
    cnt = counts[0, ROUTER_COL0:ROUTER_COL0 + N_EXPERTS].astype(jnp.int32)
    padded = (cnt + EXPERT_TM - 1) // EXPERT_TM * EXPERT_TM
    ends = jnp.cumsum(padded)
    offs = ends - padded
    e = meta[:, 0:2].astype(jnp.int32)
    rank = meta[:, 2:4].astype(jnp.int32)
    onehot = e[..., None] == jnp.arange(N_EXPERTS, dtype=jnp.int32)
    pos = jnp.sum(jnp.where(onehot, offs, 0), axis=-1) + rank
    tile_ends = ends // EXPERT_TM
    tile_ids = jnp.arange(n_tiles, dtype=jnp.int32)
    tile_expert = jnp.sum(tile_ids[:, None] >= tile_ends[None, :], axis=1).astype(jnp.int32)
    tile_expert = jnp.minimum(tile_expert, N_EXPERTS - 1)
    n_used = tile_ends[-1:].astype(jnp.int32)
    return pos.reshape(-1), tile_expert, n_used


def _col_major(w):
    return w.reshape(w.shape[0], N_COL, LANES).transpose(1, 0, 2)


def kernel(x, mem, mix_norm_g, mem_norm_g, w_in, b_gate, conv_a_w, w_out_a, b_glu, conv_b_w,
           conv_b_b, ln_b_g, ln_b_b, w_out_b, b_out_b, w_kv, w_out_x, w_o, ffn_norm_g, w_group,
           b_group, w_router, b_router, w_gate_up, w_down, final_norm_g):
    batch, seq, d = x.shape
    depth = w_in.shape[0]
    n = batch * seq
    n_tiles = (2 * n) // EXPERT_TM + N_EXPERTS
    n_slots = n_tiles * EXPERT_TM
    row = lambda a: a.reshape(1, -1)
    xf = x.reshape(n, d)
    pad = LANES - N_GROUPS - N_EXPERTS
    for l in range(depth):
        k, v = _kv_proj(mem, row(mem_norm_g[l]), w_kv[l].astype(BF16))
        h = _proj_in(xf, row(mix_norm_g[l]), w_in, l)
        x1 = _mixer(h, xf, k, v, _col_major(conv_a_w[l]), _col_major(conv_b_w[l]), row(b_glu[l]),
                    _col_major(row(conv_b_b[l])), row(ln_b_g[l]), row(ln_b_b[l]), row(b_out_b[l]),
                    row(b_gate[l]), w_out_a[l].astype(BF16), w_out_b[l].astype(BF16),
                    w_out_x[l].astype(BF16), w_o[l].astype(BF16), batch)
        w_r = jnp.pad(jnp.concatenate([w_group[l], w_router[l]], axis=1), ((0, 0), (0, pad)))
        b_r = jnp.pad(jnp.concatenate([b_group[l], b_router[l]]), (0, pad)).reshape(1, LANES)
        z, meta, counts = _route(x1, row(ffn_norm_g[l]), w_r, b_r)
        pos, tile_expert, n_used = _slot_plan(meta, counts, n_tiles)
        zs = _dispatch(pos, z, n_slots)
        y = _experts(tile_expert, n_used, zs, w_gate_up, w_down, l)
        xf = _combine(pos, x1, meta, row(final_norm_g), y, final_norm=(l == depth - 1))
    return xf.reshape(batch, seq, d)
```

```python
import functools

import jax
import jax.numpy as jnp
from jax import lax
from jax.experimental import pallas as pl
from jax.experimental.pallas import tpu as pltpu

F32 = jnp.float32
BF16 = jnp.bfloat16

D_MODEL = 1024
N_HEADS = 4
HEAD_DIM = D_MODEL // N_HEADS
MEM_LEN = 256
A_TAPS = 3
B_TAPS = 31
N_GROUPS = 4
GROUP_SIZE = 8
N_EXPERTS = N_GROUPS * GROUP_SIZE
D_FF = D_MODEL // 2
N_SEG = 9
GLU_SEGS = (3, 4)
REST_SEGS = (0, 1, 2, 5, 6, 7, 8)
EPS = 1e-6

LANES = 128
SUBLANES = 8
N_COL = D_MODEL // LANES
REST_CHUNK = len(REST_SEGS) * D_MODEL // N_COL
assert REST_CHUNK % LANES == 0

MIX_T = 256
A_HIST = 8
B_HIST = 32
A_M = (MIX_T + A_HIST) // SUBLANES
B_M = (MIX_T + B_HIST) // SUBLANES
ROUTE_T = 512
DISPATCH_T = 512
EXPERT_TM = 256
COMBINE_T = 256
ROW_UNROLL = 8
ROUTER_COL0 = N_GROUPS

VMEM_LIMIT = 56 * 1024 * 1024


def _rms(x, g):
    return x * lax.rsqrt(jnp.mean(x * x, axis=-1, keepdims=True) + EPS) * g


def _sigmoid(x):
    return 1.0 / (1.0 + jnp.exp(-x))


def _cols(x):
    return [x[:, c * LANES:(c + 1) * LANES] for c in range(N_COL)]


def _token_tile(ref, r):
    return ref.at[pl.ds(pl.multiple_of(r * N_COL, N_COL), N_COL), :]


def _load_tiled(ref, rows):
    return jnp.concatenate([ref[pl.ds(c, rows, stride=N_COL), :] for c in range(N_COL)], axis=-1)


def _store_tiled(ref, x):
    for c, part in enumerate(_cols(x)):
        ref[pl.ds(c, x.shape[0], stride=N_COL), :] = part


def _kv_kernel(mem_ref, g_ref, w_ref, k_ref, v_ref):
    mn = _rms(mem_ref[0], g_ref[...]).astype(BF16)
    kv = jnp.dot(mn, w_ref[...], preferred_element_type=F32)
    k_ref[0] = (kv[:, :D_MODEL] * (HEAD_DIM ** -0.5)).astype(BF16)
    v_ref[0] = kv[:, D_MODEL:].astype(BF16)


def _kv_proj(mem, g, w_kv):
    b = mem.shape[0]
    return pl.pallas_call(
        _kv_kernel,
        grid=(b,),
        in_specs=[
            pl.BlockSpec((1, MEM_LEN, D_MODEL), lambda i: (i, 0, 0)),
            pl.BlockSpec((1, D_MODEL), lambda i: (0, 0)),
            pl.BlockSpec((D_MODEL, 2 * D_MODEL), lambda i: (0, 0)),
        ],
        out_specs=[
            pl.BlockSpec((1, MEM_LEN, D_MODEL), lambda i: (i, 0, 0)),
            pl.BlockSpec((1, MEM_LEN, D_MODEL), lambda i: (i, 0, 0)),
        ],
        out_shape=[jax.ShapeDtypeStruct((b, MEM_LEN, D_MODEL), BF16)] * 2,
        compiler_params=pltpu.CompilerParams(vmem_limit_bytes=VMEM_LIMIT),
        name="kv_proj",
    )(mem, g, w_kv)


def _causal_conv(src, wrap, dst, taps, bias, col, m, n_taps):
    first_wrap = m - (n_taps - 1)
    view = lambda ref, j: ref[col, pl.ds(j, SUBLANES, stride=m), :]
    for j in range(first_wrap, m):
        wrap[col, pl.ds((j - first_wrap) * SUBLANES, SUBLANES), :] = pltpu.roll(view(src, j), 1, 0)
    w = [jnp.broadcast_to(taps[col, k:k + 1, :], (SUBLANES, LANES)) for k in range(n_taps)]
    for i in range(m):
        acc = None if bias is None else jnp.broadcast_to(bias[col], (SUBLANES, LANES))
        for s in range(n_taps):
            j = i - s
            if j >= 0:
                x = view(src, j)
            else:
                x = wrap[col, pl.ds((j + m - first_wrap) * SUBLANES, SUBLANES), :]
            term = w[n_taps - 1 - s] * x
            acc = term if acc is None else acc + term
        dst[col, pl.ds(i, SUBLANES, stride=m), :] = acc


def _mixer_kernel(x_ref, k_ref, v_ref, gin, wglu, wrest, caw, cbw, bglu, cbb, lng, lnb, bob, bgate,
                  woa, wob, wox, wo, o_ref, xn_ref, hbuf, a_src, a_wrap, a_dst, b_src, b_wrap, b_dst):
    t = MIX_T

    @pl.when(pl.program_id(1) == 0)
    def _():
        a_src[:, 0:A_HIST, :] = jnp.zeros((N_COL, A_HIST, LANES), F32)
        b_src[:, 0:B_HIST, :] = jnp.zeros((N_COL, B_HIST, LANES), F32)

    xn_ref[...] = _rms(x_ref[...], gin[...]).astype(BF16)

    glu = jnp.dot(xn_ref[...], wglu[...], preferred_element_type=F32) + bglu[...]
    for c, part in enumerate(_cols(glu[:, 0:D_MODEL] * _sigmoid(glu[:, D_MODEL:2 * D_MODEL]))):
        b_src[c, B_HIST:B_HIST + t, :] = part

    def conv_and_project(c, carry):
        _causal_conv(b_src, b_wrap, b_dst, cbw, cbb, c, B_M, B_TAPS)
        hbuf[c] = jnp.dot(xn_ref[...], wrest[c], preferred_element_type=F32)
        return carry

    lax.fori_loop(0, N_COL, conv_and_project, 0)

    def rest(r):
        parts, g = [], r * D_MODEL
        while g < (r + 1) * D_MODEL:
            chunk, off = divmod(g, REST_CHUNK)
            width = min(REST_CHUNK - off, (r + 1) * D_MODEL - g)
            parts.append(hbuf[chunk, :, off:off + width])
            g += width
        return jnp.concatenate(parts, axis=-1)

    segs = {s: functools.partial(rest, r) for r, s in enumerate(REST_SEGS)}
    conv_b = jnp.concatenate([b_dst[c, B_HIST:B_HIST + t, :] for c in range(N_COL)], axis=-1)
    b_src[:, 0:B_HIST, :] = b_src[:, t:t + B_HIST, :]
    mu = jnp.mean(conv_b, axis=-1, keepdims=True)
    xc = conv_b - mu
    var = jnp.mean(xc * xc, axis=-1, keepdims=True)
    ln = xc * lax.rsqrt(var + EPS) * lng[...] + lnb[...]
    y_b = jnp.dot((ln * _sigmoid(ln)).astype(BF16), wob[...], preferred_element_type=F32) + bob[...]
    merged = _sigmoid(segs[7]() + bgate[:, D_MODEL:2 * D_MODEL]) * y_b

    for c, part in enumerate(_cols(segs[2]() * segs[0]())):
        a_src[c, A_HIST:A_HIST + t, :] = part
    for c in range(N_COL):
        _causal_conv(a_src, a_wrap, a_dst, caw, None, c, A_M, A_TAPS)
    conv_a = jnp.concatenate([a_dst[c, A_HIST:A_HIST + t, :] for c in range(N_COL)], axis=-1)
    a_src[:, 0:A_HIST, :] = a_src[:, t:t + A_HIST, :]
    y_a = jnp.dot((segs[1]() * conv_a).astype(BF16), woa[...], preferred_element_type=F32)
    merged += _sigmoid(segs[6]() + bgate[:, 0:D_MODEL]) * y_a

    q = segs[5]().astype(BF16)
    heads = []
    for h in range(N_HEADS):
        sl = slice(h * HEAD_DIM, (h + 1) * HEAD_DIM)
        s = lax.dot_general(q[:, sl], k_ref[0, :, sl], (((1,), (1,)), ((), ())),
                            preferred_element_type=F32)
        p = jnp.exp(s - jnp.max(s, axis=-1, keepdims=True))
        denom = jnp.sum(p, axis=-1, keepdims=True)
        o = jnp.dot(p.astype(BF16), v_ref[0, :, sl], preferred_element_type=F32)
        heads.append((o / denom).astype(BF16))
    attn = jnp.concatenate(heads, axis=-1)
    y_x = jnp.dot(attn, wox[...], preferred_element_type=F32)
    merged += _sigmoid(segs[8]() + bgate[:, 2 * D_MODEL:3 * D_MODEL]) * y_x

    o_ref[...] = x_ref[...] + jnp.dot(merged.astype(BF16), wo[...], preferred_element_type=F32)


def _split_w_in(w):
    seg = lambda s: w[:, s * D_MODEL:(s + 1) * D_MODEL]
    wglu = jnp.concatenate([seg(s) for s in GLU_SEGS], axis=1).astype(BF16)
    wrest = jnp.concatenate([seg(s) for s in REST_SEGS], axis=1).astype(BF16)
    wrest = wrest.reshape(D_MODEL, N_COL, REST_CHUNK).transpose(1, 0, 2)
    return wglu, wrest


def _mixer(x, k, v, gin, w_in, caw, cbw, bglu, cbb, lng, lnb, bob, bgate, woa, wob, wox, wo, batch):
    wglu, wrest = _split_w_in(w_in)
    n = x.shape[0]
    n_s = n // batch // MIX_T
    row = lambda b, s: b * n_s + s
    const2 = lambda shape: pl.BlockSpec(shape, lambda b, s: (0, 0))
    const3 = lambda shape: pl.BlockSpec(shape, lambda b, s: (0, 0, 0))
    resident = lambda shape: pl.BlockSpec(shape, lambda b, s: (0, 0), pipeline_mode=pl.Buffered(1))
    kv_spec = pl.BlockSpec((1, MEM_LEN, D_MODEL), lambda b, s: (b, 0, 0))
    w_spec = resident((D_MODEL, D_MODEL))
    col_buf = lambda rows: pltpu.VMEM((N_COL, rows, LANES), F32)
    return pl.pallas_call(
        _mixer_kernel,
        grid=(batch, n_s),
        in_specs=[
            pl.BlockSpec((MIX_T, D_MODEL), lambda b, s: (row(b, s), 0)),
            kv_spec, kv_spec,
            const2((1, D_MODEL)), resident((D_MODEL, 2 * D_MODEL)),
            pl.BlockSpec((N_COL, D_MODEL, REST_CHUNK), lambda b, s: (0, 0, 0),
                         pipeline_mode=pl.Buffered(1)),
            const3((N_COL, A_TAPS, LANES)), const3((N_COL, B_TAPS, LANES)),
            const2((1, 2 * D_MODEL)), const3((N_COL, 1, LANES)), const2((1, D_MODEL)),
            const2((1, D_MODEL)), const2((1, D_MODEL)), const2((1, 3 * D_MODEL)),
            w_spec, w_spec, w_spec, w_spec,
        ],
        out_specs=pl.BlockSpec((MIX_T, D_MODEL), lambda b, s: (row(b, s), 0)),
        out_shape=jax.ShapeDtypeStruct((n, D_MODEL), F32),
        scratch_shapes=[
            pltpu.VMEM((MIX_T, D_MODEL), BF16), pltpu.VMEM((N_COL, MIX_T, REST_CHUNK), F32),
            col_buf(SUBLANES * A_M), col_buf(SUBLANES * (A_TAPS - 1)), col_buf(SUBLANES * A_M),
            col_buf(SUBLANES * B_M), col_buf(SUBLANES * (B_TAPS - 1)), col_buf(SUBLANES * B_M),
        ],
        compiler_params=pltpu.CompilerParams(
            dimension_semantics=("arbitrary", "arbitrary"), vmem_limit_bytes=VMEM_LIMIT),
        name="mixer",
    )(x, k, v, gin, wglu, wrest, caw, cbw, bglu, cbb, lng, lnb, bob, bgate, woa, wob, wox, wo)


def _route_kernel(x_ref, g_ref, w_ref, b_ref, z_ref, meta_ref, cnt_ref, cnt_acc):
    t = ROUTE_T

    @pl.when(pl.program_id(0) == 0)
    def _():
        cnt_acc[...] = jnp.zeros_like(cnt_acc)

    z = _rms(x_ref[...], g_ref[...])
    _store_tiled(z_ref, z)
    logits = jnp.dot(z, w_ref[...], preferred_element_type=F32,
                     precision=lax.Precision.HIGHEST) + b_ref[...]
    lane = lax.broadcasted_iota(jnp.int32, (t, LANES), 1).astype(F32)
    neg = jnp.float32(-jnp.inf)
    big = jnp.float32(1e9)

    gmask = lane < N_GROUPS
    gl = jnp.where(gmask, logits, neg)
    gmax = jnp.max(gl, axis=-1, keepdims=True)
    g_sel = jnp.min(jnp.where(gl == gmax, lane, big), axis=-1, keepdims=True)
    g_p = 1.0 / jnp.sum(jnp.where(gmask, jnp.exp(logits - gmax), 0.0), axis=-1, keepdims=True)

    lo = ROUTER_COL0 + GROUP_SIZE * g_sel
    el = jnp.where((lane >= lo) & (lane < lo + GROUP_SIZE), logits, neg)
    m1 = jnp.max(el, axis=-1, keepdims=True)
    i1 = jnp.min(jnp.where(el == m1, lane, big), axis=-1, keepdims=True)
    el2 = jnp.where(lane == i1, neg, el)
    m2 = jnp.max(el2, axis=-1, keepdims=True)
    i2 = jnp.min(jnp.where(el2 == m2, lane, big), axis=-1, keepdims=True)
    r = jnp.exp(m2 - m1)
    c1 = g_p / (1.0 + r)
    c2 = g_p * r / (1.0 + r)

    oh1 = lane == i1
    oh2 = lane == i2
    ri = lax.broadcasted_iota(jnp.int32, (t, t), 0)
    ci = lax.broadcasted_iota(jnp.int32, (t, t), 1)
    tri = jnp.where(ci < ri, 1.0, 0.0).astype(BF16)
    oh1f = jnp.where(oh1, 1.0, 0.0)
    oh2f = jnp.where(oh2, 1.0, 0.0)
    pre1 = jnp.dot(tri, oh1f.astype(BF16), preferred_element_type=F32)
    pre2 = jnp.dot(tri, oh2f.astype(BF16), preferred_element_type=F32)
    tot1 = jnp.sum(oh1f, axis=0, keepdims=True)
    tot2 = jnp.sum(oh2f, axis=0, keepdims=True)
    cnt = cnt_acc[...]
    rank1 = jnp.sum(oh1f * (pre1 + cnt), axis=-1, keepdims=True)
    rank2 = jnp.sum(oh2f * (pre2 + tot1 + cnt), axis=-1, keepdims=True)
    cnt_new = cnt + tot1 + tot2
    cnt_acc[...] = cnt_new
    cnt_ref[...] = cnt_new

    meta = jnp.where(lane == 0, i1 - ROUTER_COL0, 0.0)
    meta = jnp.where(lane == 1, i2 - ROUTER_COL0, meta)
    meta = jnp.where(lane == 2, rank1, meta)
    meta = jnp.where(lane == 3, rank2, meta)
    meta = jnp.where(lane == 4, c1, meta)
    meta = jnp.where(lane == 5, c2, meta)
    meta_ref[...] = meta


def _route(x, g, w, b):
    n = x.shape[0]
    return pl.pallas_call(
        _route_kernel,
        grid=(n // ROUTE_T,),
        in_specs=[
            pl.BlockSpec((ROUTE_T, D_MODEL), lambda i: (i, 0)),
            pl.BlockSpec((1, D_MODEL), lambda i: (0, 0)),
            pl.BlockSpec((D_MODEL, LANES), lambda i: (0, 0)),
            pl.BlockSpec((1, LANES), lambda i: (0, 0)),
        ],
        out_specs=[
            pl.BlockSpec((ROUTE_T * N_COL, LANES), lambda i: (i, 0)),
            pl.BlockSpec((ROUTE_T, LANES), lambda i: (i, 0)),
            pl.BlockSpec((1, LANES), lambda i: (0, 0)),
        ],
        out_shape=[
            jax.ShapeDtypeStruct((n * N_COL, LANES), F32),
            jax.ShapeDtypeStruct((n, LANES), F32),
            jax.ShapeDtypeStruct((1, LANES), F32),
        ],
        scratch_shapes=[pltpu.VMEM((1, LANES), F32)],
        compiler_params=pltpu.CompilerParams(
            dimension_semantics=("arbitrary",), vmem_limit_bytes=VMEM_LIMIT),
        name="route",
    )(x, g, w, b)


def _zero_pad_slots(pad_start_ref, pad_len_ref, n_used_ref, zeros_ref, out_ref, sem):
    half = EXPERT_TM // 2
    sizes = [1 << b for b in reversed(range(EXPERT_TM.bit_length() - 1))]
    n_tiles = out_ref.shape[0] // (EXPERT_TM * N_COL)

    def zero_copy(at, size):
        return pltpu.make_async_copy(
            zeros_ref.at[pl.ds(0, size * N_COL), :],
            out_ref.at[pl.ds(pl.multiple_of(at * N_COL, N_COL), size * N_COL), :], sem)

    def pad_copies(e, act):
        at = pad_start_ref[e]
        for size in sizes:
            take = (pad_len_ref[e] & size) != 0

            @pl.when(take)
            def _(at=at, size=size):
                act(zero_copy(at, size))

            at = at + jnp.where(take, size, 0)

    def tail_copies(j, act):
        act(zero_copy(j * EXPERT_TM, half))
        act(zero_copy(j * EXPERT_TM + half, half))

    def loops(act):
        def pad_body(e, carry):
            pad_copies(e, act)
            return carry

        def tail_body(j, carry):
            tail_copies(j, act)
            return carry

        lax.fori_loop(0, N_EXPERTS, pad_body, 0)
        lax.fori_loop(n_used_ref[0], n_tiles, tail_body, 0)

    loops(lambda cp: cp.start())
    loops(lambda cp: cp.wait())


def _dispatch_kernel(pos_ref, pad_start_ref, pad_len_ref, n_used_ref, z_ref, out_ref, zeros_ref,
                     sem, pad_sem):
    @pl.when(pl.program_id(0) == 0)
    def _():
        zeros_ref[...] = jnp.zeros_like(zeros_ref)
        _zero_pad_slots(pad_start_ref, pad_len_ref, n_used_ref, zeros_ref, out_ref, pad_sem)

    def start(g, carry):
        for u in range(ROW_UNROLL):
            r = g * ROW_UNROLL + u
            for k in range(2):
                pltpu.make_async_copy(
                    _token_tile(z_ref, r), _token_tile(out_ref, pos_ref[2 * r + k]), sem).start()
        return carry

    lax.fori_loop(0, DISPATCH_T // ROW_UNROLL, start, 0)
    for _ in range(2):
        pltpu.make_async_copy(z_ref, out_ref.at[pl.ds(0, DISPATCH_T * N_COL), :], sem).wait()


def _dispatch(pos_flat, pad_start, pad_len, n_used, z, n_slots):
    n = z.shape[0] // N_COL
    return pl.pallas_call(
        _dispatch_kernel,
        grid=(n // DISPATCH_T,),
        in_specs=[
            pl.BlockSpec((2 * DISPATCH_T,), lambda i: (i,), memory_space=pltpu.SMEM),
            pl.BlockSpec(memory_space=pltpu.SMEM),
            pl.BlockSpec(memory_space=pltpu.SMEM),
            pl.BlockSpec(memory_space=pltpu.SMEM),
            pl.BlockSpec((DISPATCH_T * N_COL, LANES), lambda i: (i, 0)),
        ],
        out_specs=pl.BlockSpec(memory_space=pl.ANY),
        out_shape=jax.ShapeDtypeStruct((n_slots * N_COL, LANES), F32),
        scratch_shapes=[
            pltpu.VMEM((EXPERT_TM // 2 * N_COL, LANES), F32),
            pltpu.SemaphoreType.DMA,
            pltpu.SemaphoreType.DMA,
        ],
        compiler_params=pltpu.CompilerParams(
            dimension_semantics=("arbitrary",), vmem_limit_bytes=VMEM_LIMIT),
        name="dispatch",
    )(pos_flat, pad_start, pad_len, n_used, z)


def _experts_kernel(te_ref, nu_ref, z_ref, wgu_ref, wd_ref, y_ref, wgu_bf, wd_bf):
    j = pl.program_id(0)
    used = j < nu_ref[0]
    new_expert = (j == 0) | (te_ref[j] != te_ref[jnp.maximum(j - 1, 0)])

    @pl.when(used & new_expert)
    def _():
        wgu_bf[...] = wgu_ref[0, 0].astype(BF16)
        wd_bf[...] = wd_ref[0, 0].astype(BF16)

    @pl.when(used)
    def _():
        z = _load_tiled(z_ref, EXPERT_TM).astype(BF16)
        gu = jnp.dot(z, wgu_bf[...], preferred_element_type=F32)
        gate = gu[:, :D_FF]
        up = gu[:, D_FF:]
        hmid = (gate * _sigmoid(gate) * up).astype(BF16)
        _store_tiled(y_ref, jnp.dot(hmid, wd_bf[...], preferred_element_type=F32))

    @pl.when(jnp.logical_not(used))
    def _():
        y_ref[...] = jnp.zeros_like(y_ref)


def _experts(tile_expert, n_used, zs, wgu_all, wd_all, layer):
    n_slots = zs.shape[0] // N_COL
    n_tiles = n_slots // EXPERT_TM
    last = lambda j, nu: jnp.maximum(jnp.minimum(j, nu[0] - 1), 0)
    return pl.pallas_call(
        _experts_kernel,
        grid_spec=pltpu.PrefetchScalarGridSpec(
            num_scalar_prefetch=2,
            grid=(n_tiles,),
            in_specs=[
                pl.BlockSpec((EXPERT_TM * N_COL, LANES), lambda j, te, nu: (last(j, nu), 0)),
                pl.BlockSpec((1, 1, D_MODEL, 2 * D_FF), lambda j, te, nu: (layer, te[j], 0, 0)),
                pl.BlockSpec((1, 1, D_FF, D_MODEL), lambda j, te, nu: (layer, te[j], 0, 0)),
            ],
            out_specs=pl.BlockSpec((EXPERT_TM * N_COL, LANES), lambda j, te, nu: (j, 0)),
            scratch_shapes=[pltpu.VMEM((D_MODEL, 2 * D_FF), BF16), pltpu.VMEM((D_FF, D_MODEL), BF16)],
        ),
        out_shape=jax.ShapeDtypeStruct((n_slots * N_COL, LANES), F32),
        compiler_params=pltpu.CompilerParams(
            dimension_semantics=("arbitrary",), vmem_limit_bytes=VMEM_LIMIT),
        name="experts",
    )(tile_expert, n_used, zs, wgu_all, wd_all)


def _combine_kernel(pos_ref, pos_next_ref, x_ref, meta_ref, g_ref, y_ref, o_ref, buf0, buf1, sems,
                    *, final_norm):
    bufs = (buf0, buf1)
    i = pl.program_id(0)
    slot = i % 2

    def gather(pos, into):
        def start(g, carry):
            for u in range(ROW_UNROLL):
                r = g * ROW_UNROLL + u
                for k in range(2):
                    pltpu.make_async_copy(_token_tile(y_ref, pos[2 * r + k]),
                                          _token_tile(bufs[k].at[into], r), sems.at[into]).start()
            return carry

        lax.fori_loop(0, COMBINE_T // ROW_UNROLL, start, 0)

    @pl.when(i == 0)
    def _():
        gather(pos_ref, 0)

    @pl.when(i + 1 < pl.num_programs(0))
    def _():
        gather(pos_next_ref, 1 - slot)

    for k in range(2):
        pltpu.make_async_copy(y_ref.at[pl.ds(0, COMBINE_T * N_COL), :], bufs[k].at[slot],
                              sems.at[slot]).wait()
    meta = meta_ref[...]
    y0 = _load_tiled(buf0.at[slot], COMBINE_T)
    y1 = _load_tiled(buf1.at[slot], COMBINE_T)
    out = x_ref[...] + meta[:, 4:5] * y0 + meta[:, 5:6] * y1
    if final_norm:
        out = _rms(out, g_ref[...])
    o_ref[...] = out


def _combine(pos_flat, x, meta, g, y, final_norm):
    n = x.shape[0]
    n_steps = n // COMBINE_T
    return pl.pallas_call(
        functools.partial(_combine_kernel, final_norm=final_norm),
        grid=(n_steps,),
        in_specs=[
            pl.BlockSpec((2 * COMBINE_T,), lambda i: (i,), memory_space=pltpu.SMEM),
            pl.BlockSpec((2 * COMBINE_T,), lambda i: (jnp.minimum(i + 1, n_steps - 1),),
                         memory_space=pltpu.SMEM),
            pl.BlockSpec((COMBINE_T, D_MODEL), lambda i: (i, 0)),
            pl.BlockSpec((COMBINE_T, LANES), lambda i: (i, 0)),
            pl.BlockSpec((1, D_MODEL), lambda i: (0, 0)),
            pl.BlockSpec(memory_space=pl.ANY),
        ],
        out_specs=pl.BlockSpec((COMBINE_T, D_MODEL), lambda i: (i, 0)),
        out_shape=jax.ShapeDtypeStruct((n, D_MODEL), F32),
        scratch_shapes=[
            pltpu.VMEM((2, COMBINE_T * N_COL, LANES), F32),
            pltpu.VMEM((2, COMBINE_T * N_COL, LANES), F32),
            pltpu.SemaphoreType.DMA((2,)),
        ],
        compiler_params=pltpu.CompilerParams(
            dimension_semantics=("arbitrary",), vmem_limit_bytes=VMEM_LIMIT),
        name="combine",
    )(pos_flat, pos_flat, x, meta, g, y)


def _slot_plan(meta, counts, n_tiles):
    cnt = counts[0, ROUTER_COL0:ROUTER_COL0 + N_EXPERTS].astype(jnp.int32)
    padded = (cnt + EXPERT_TM - 1) // EXPERT_TM * EXPERT_TM
    ends = jnp.cumsum(padded)
    offs = ends - padded
    e = meta[:, 0:2].astype(jnp.int32)
    rank = meta[:, 2:4].astype(jnp.int32)
    onehot = e[..., None] == jnp.arange(N_EXPERTS, dtype=jnp.int32)
    pos = jnp.sum(jnp.where(onehot, offs, 0), axis=-1) + rank
    tile_ends = ends // EXPERT_TM
    tile_ids = jnp.arange(n_tiles, dtype=jnp.int32)
    tile_expert = jnp.sum(tile_ids[:, None] >= tile_ends[None, :], axis=1).astype(jnp.int32)
    tile_expert = jnp.minimum(tile_expert, N_EXPERTS - 1)
    n_used = tile_ends[-1:].astype(jnp.int32)
    return pos.reshape(-1), tile_expert, n_used, offs + cnt, padded - cnt


def _col_major(w):
    return w.reshape(w.shape[0], N_COL, LANES).transpose(1, 0, 2)


def kernel(x, mem, mix_norm_g, mem_norm_g, w_in, b_gate, conv_a_w, w_out_a, b_glu, conv_b_w,
           conv_b_b, ln_b_g, ln_b_b, w_out_b, b_out_b, w_kv, w_out_x, w_o, ffn_norm_g, w_group,
           b_group, w_router, b_router, w_gate_up, w_down, final_norm_g):
    batch, seq, d = x.shape
    depth = w_in.shape[0]
    n = batch * seq
    n_tiles = (2 * n) // EXPERT_TM + N_EXPERTS
    n_slots = n_tiles * EXPERT_TM
    row = lambda a: a.reshape(1, -1)
    xf = x.reshape(n, d)
    pad = LANES - N_GROUPS - N_EXPERTS
    for l in range(depth):
        k, v = _kv_proj(mem, row(mem_norm_g[l]), w_kv[l].astype(BF16))
        x1 = _mixer(xf, k, v, row(mix_norm_g[l]), w_in[l],
                    _col_major(conv_a_w[l]), _col_major(conv_b_w[l]), row(b_glu[l]),
                    _col_major(row(conv_b_b[l])), row(ln_b_g[l]), row(ln_b_b[l]), row(b_out_b[l]),
                    row(b_gate[l]), w_out_a[l].astype(BF16), w_out_b[l].astype(BF16),
                    w_out_x[l].astype(BF16), w_o[l].astype(BF16), batch)
        w_r = jnp.pad(jnp.concatenate([w_group[l], w_router[l]], axis=1), ((0, 0), (0, pad)))
        b_r = jnp.pad(jnp.concatenate([b_group[l], b_router[l]]), (0, pad)).reshape(1, LANES)
        z, meta, counts = _route(x1, row(ffn_norm_g[l]), w_r, b_r)
        pos, tile_expert, n_used, pad_start, pad_len = _slot_plan(meta, counts, n_tiles)
        zs = _dispatch(pos, pad_start, pad_len, n_used, z, n_slots)
        y = _experts(tile_expert, n_used, zs, w_gate_up, w_down, l)
        xf = _combine(pos, x1, meta, row(final_norm_g), y, final_norm=(l == depth - 1))
    return xf.reshape(batch, seq, d)
```

```python
import functools

import jax
import jax.numpy as jnp
from jax import lax
from jax.experimental import pallas as pl
from jax.experimental.pallas import tpu as pltpu

F32 = jnp.float32
BF16 = jnp.bfloat16

D_MODEL = 1024
N_HEADS = 4
HEAD_DIM = D_MODEL // N_HEADS
MEM_LEN = 256
A_TAPS = 3
B_TAPS = 31
N_GROUPS = 4
GROUP_SIZE = 8
N_EXPERTS = N_GROUPS * GROUP_SIZE
D_FF = D_MODEL // 2
N_SEG = 9
EPS = 1e-6

LANES = 128
SUBLANES = 8
N_COL = D_MODEL // LANES

PROJ_TM = 2048
PROJ_TN = 1024
MIX_T = 512
A_HIST = 8
B_HIST = 32
A_M = (MIX_T + A_HIST) // SUBLANES
B_M = (MIX_T + B_HIST) // SUBLANES
ROUTE_T = 512
DISPATCH_T = 512
EXPERT_TM = 512
COMBINE_T = 256
ROW_UNROLL = 8
ROUTER_COL0 = N_GROUPS

VMEM_LIMIT = 56 * 1024 * 1024


def _rms(x, g):
    return x * lax.rsqrt(jnp.mean(x * x, axis=-1, keepdims=True) + EPS) * g


def _sigmoid(x):
    return 1.0 / (1.0 + jnp.exp(-x))


def _cols(x):
    return [x[:, c * LANES:(c + 1) * LANES] for c in range(N_COL)]


def _token_tile(ref, r):
    return ref.at[pl.ds(pl.multiple_of(r * N_COL, N_COL), N_COL), :]


def _load_tiled(ref, rows):
    return jnp.concatenate([ref[pl.ds(c, rows, stride=N_COL), :] for c in range(N_COL)], axis=-1)


def _store_tiled(ref, x):
    for c, part in enumerate(_cols(x)):
        ref[pl.ds(c, x.shape[0], stride=N_COL), :] = part


def _kv_kernel(mem_ref, g_ref, w_ref, k_ref, v_ref):
    mn = _rms(mem_ref[0], g_ref[...]).astype(BF16)
    kv = jnp.dot(mn, w_ref[...], preferred_element_type=F32)
    k_ref[0] = (kv[:, :D_MODEL] * (HEAD_DIM ** -0.5)).astype(BF16)
    v_ref[0] = kv[:, D_MODEL:].astype(BF16)


def _kv_proj(mem, g, w_kv):
    b = mem.shape[0]
    return pl.pallas_call(
        _kv_kernel,
        grid=(b,),
        in_specs=[
            pl.BlockSpec((1, MEM_LEN, D_MODEL), lambda i: (i, 0, 0)),
            pl.BlockSpec((1, D_MODEL), lambda i: (0, 0)),
            pl.BlockSpec((D_MODEL, 2 * D_MODEL), lambda i: (0, 0)),
        ],
        out_specs=[
            pl.BlockSpec((1, MEM_LEN, D_MODEL), lambda i: (i, 0, 0)),
            pl.BlockSpec((1, MEM_LEN, D_MODEL), lambda i: (i, 0, 0)),
        ],
        out_shape=[jax.ShapeDtypeStruct((b, MEM_LEN, D_MODEL), BF16)] * 2,
        compiler_params=pltpu.CompilerParams(vmem_limit_bytes=VMEM_LIMIT),
        name="kv_proj",
    )(mem, g, w_kv)


def _proj_in_kernel(x_ref, g_ref, w_ref, o_ref, xn_ref):
    @pl.when(pl.program_id(1) == 0)
    def _():
        xn_ref[...] = _rms(x_ref[...], g_ref[...]).astype(BF16)

    o_ref[...] = jnp.dot(xn_ref[...], w_ref[0].astype(BF16),
                         preferred_element_type=F32).astype(BF16)


def _proj_in(x, g, w_all, layer):
    n = x.shape[0]
    n_in = w_all.shape[2]
    return pl.pallas_call(
        _proj_in_kernel,
        grid=(n // PROJ_TM, n_in // PROJ_TN),
        in_specs=[
            pl.BlockSpec((PROJ_TM, D_MODEL), lambda i, j: (i, 0)),
            pl.BlockSpec((1, D_MODEL), lambda i, j: (0, 0)),
            pl.BlockSpec((1, D_MODEL, PROJ_TN), lambda i, j: (layer, 0, j)),
        ],
        out_specs=pl.BlockSpec((PROJ_TM, PROJ_TN), lambda i, j: (i, j)),
        out_shape=jax.ShapeDtypeStruct((n, n_in), BF16),
        scratch_shapes=[pltpu.VMEM((PROJ_TM, D_MODEL), BF16)],
        compiler_params=pltpu.CompilerParams(
            dimension_semantics=("arbitrary", "arbitrary"), vmem_limit_bytes=VMEM_LIMIT),
        name="proj_in",
    )(x, g, w_all)


def _causal_conv(src, wrap, dst, taps, bias, col, m, n_taps):
    first_wrap = m - (n_taps - 1)
    view = lambda ref, j: ref[col, pl.ds(j, SUBLANES, stride=m), :]
    for j in range(first_wrap, m):
        wrap[col, pl.ds((j - first_wrap) * SUBLANES, SUBLANES), :] = pltpu.roll(view(src, j), 1, 0)
    w = [jnp.broadcast_to(taps[col, k:k + 1, :], (SUBLANES, LANES)) for k in range(n_taps)]
    for i in range(m):
        acc = None if bias is None else jnp.broadcast_to(bias[col], (SUBLANES, LANES))
        for s in range(n_taps):
            j = i - s
            if j >= 0:
                x = view(src, j)
            else:
                x = wrap[col, pl.ds((j + m - first_wrap) * SUBLANES, SUBLANES), :]
            term = w[n_taps - 1 - s] * x
            acc = term if acc is None else acc + term
        dst[col, pl.ds(i, SUBLANES, stride=m), :] = acc


def _mixer_kernel(aval, agb, agc, gluv, glug, q_ref, g0, g1, g2, x_ref, k_ref, v_ref,
                  caw, cbw, bglu, cbb, lng, lnb, bob, bgate, woa, wob, wox, wo,
                  o_ref, a_src, a_wrap, a_dst, b_src, b_wrap, b_dst):
    t = MIX_T

    @pl.when(pl.program_id(1) == 0)
    def _():
        a_src[:, 0:A_HIST, :] = jnp.zeros((N_COL, A_HIST, LANES), F32)
        b_src[:, 0:B_HIST, :] = jnp.zeros((N_COL, B_HIST, LANES), F32)

    cv = agc[...].astype(F32) * aval[...].astype(F32)
    for c, part in enumerate(_cols(cv)):
        a_src[c, A_HIST:A_HIST + t, :] = part
    for c in range(N_COL):
        _causal_conv(a_src, a_wrap, a_dst, caw, None, c, A_M, A_TAPS)
    conv_a = jnp.concatenate([a_dst[c, A_HIST:A_HIST + t, :] for c in range(N_COL)], axis=-1)
    a_src[:, 0:A_HIST, :] = a_src[:, t:t + A_HIST, :]
    ya_in = (agb[...].astype(F32) * conv_a).astype(BF16)
    y_a = jnp.dot(ya_in, woa[...], preferred_element_type=F32)
    merged = _sigmoid(g0[...].astype(F32) + bgate[:, 0:D_MODEL]) * y_a

    gv = gluv[...].astype(F32) + bglu[:, 0:D_MODEL]
    gg = glug[...].astype(F32) + bglu[:, D_MODEL:2 * D_MODEL]
    for c, part in enumerate(_cols(gv * _sigmoid(gg))):
        b_src[c, B_HIST:B_HIST + t, :] = part

    def conv_col(c, carry):
        _causal_conv(b_src, b_wrap, b_dst, cbw, cbb, c, B_M, B_TAPS)
        return carry

    lax.fori_loop(0, N_COL, conv_col, 0)
    conv_b = jnp.concatenate([b_dst[c, B_HIST:B_HIST + t, :] for c in range(N_COL)], axis=-1)
    b_src[:, 0:B_HIST, :] = b_src[:, t:t + B_HIST, :]
    mu = jnp.mean(conv_b, axis=-1, keepdims=True)
    xc = conv_b - mu
    var = jnp.mean(xc * xc, axis=-1, keepdims=True)
    ln = xc * lax.rsqrt(var + EPS) * lng[...] + lnb[...]
    y_b = jnp.dot((ln * _sigmoid(ln)).astype(BF16), wob[...], preferred_element_type=F32) + bob[...]
    merged += _sigmoid(g1[...].astype(F32) + bgate[:, D_MODEL:2 * D_MODEL]) * y_b

    heads = []
    for h in range(N_HEADS):
        sl = slice(h * HEAD_DIM, (h + 1) * HEAD_DIM)
        s = lax.dot_general(q_ref[:, sl], k_ref[0, :, sl], (((1,), (1,)), ((), ())),
                            preferred_element_type=F32)
        p = jnp.exp(s - jnp.max(s, axis=-1, keepdims=True))
        denom = jnp.sum(p, axis=-1, keepdims=True)
        o = jnp.dot(p.astype(BF16), v_ref[0, :, sl], preferred_element_type=F32)
        heads.append((o / denom).astype(BF16))
    attn = jnp.concatenate(heads, axis=-1)
    y_x = jnp.dot(attn, wox[...], preferred_element_type=F32)
    merged += _sigmoid(g2[...].astype(F32) + bgate[:, 2 * D_MODEL:3 * D_MODEL]) * y_x

    o_ref[...] = x_ref[...] + jnp.dot(merged.astype(BF16), wo[...], preferred_element_type=F32)


def _mixer(h, x, k, v, caw, cbw, bglu, cbb, lng, lnb, bob, bgate, woa, wob, wox, wo, batch):
    n = x.shape[0]
    n_s = n // batch // MIX_T
    row = lambda b, s: b * n_s + s
    seg_specs = [pl.BlockSpec((MIX_T, D_MODEL), functools.partial(lambda b, s, c: (row(b, s), c), c=c))
                 for c in range(N_SEG)]
    const2 = lambda shape: pl.BlockSpec(shape, lambda b, s: (0, 0))
    const3 = lambda shape: pl.BlockSpec(shape, lambda b, s: (0, 0, 0))
    kv_spec = pl.BlockSpec((1, MEM_LEN, D_MODEL), lambda b, s: (b, 0, 0))
    w_spec = pl.BlockSpec((D_MODEL, D_MODEL), lambda b, s: (0, 0), pipeline_mode=pl.Buffered(1))
    col_buf = lambda rows: pltpu.VMEM((N_COL, rows, LANES), F32)
    return pl.pallas_call(
        _mixer_kernel,
        grid=(batch, n_s),
        in_specs=seg_specs + [
            pl.BlockSpec((MIX_T, D_MODEL), lambda b, s: (row(b, s), 0)),
            kv_spec, kv_spec,
            const3((N_COL, A_TAPS, LANES)), const3((N_COL, B_TAPS, LANES)),
            const2((1, 2 * D_MODEL)), const3((N_COL, 1, LANES)), const2((1, D_MODEL)),
            const2((1, D_MODEL)), const2((1, D_MODEL)), const2((1, 3 * D_MODEL)),
            w_spec, w_spec, w_spec, w_spec,
        ],
        out_specs=pl.BlockSpec((MIX_T, D_MODEL), lambda b, s: (row(b, s), 0)),
        out_shape=jax.ShapeDtypeStruct((n, D_MODEL), F32),
        scratch_shapes=[
            col_buf(SUBLANES * A_M), col_buf(SUBLANES * (A_TAPS - 1)), col_buf(SUBLANES * A_M),
            col_buf(SUBLANES * B_M), col_buf(SUBLANES * (B_TAPS - 1)), col_buf(SUBLANES * B_M),
        ],
        compiler_params=pltpu.CompilerParams(
            dimension_semantics=("arbitrary", "arbitrary"), vmem_limit_bytes=VMEM_LIMIT),
        name="mixer",
    )(*([h] * N_SEG), x, k, v, caw, cbw, bglu, cbb, lng, lnb, bob, bgate, woa, wob, wox, wo)


def _route_kernel(x_ref, g_ref, w_ref, b_ref, z_ref, meta_ref, cnt_ref, cnt_acc):
    t = ROUTE_T

    @pl.when(pl.program_id(0) == 0)
    def _():
        cnt_acc[...] = jnp.zeros_like(cnt_acc)

    z = _rms(x_ref[...], g_ref[...])
    _store_tiled(z_ref, z)
    logits = jnp.dot(z, w_ref[...], preferred_element_type=F32,
                     precision=lax.Precision.HIGHEST) + b_ref[...]
    lane = lax.broadcasted_iota(jnp.int32, (t, LANES), 1).astype(F32)
    neg = jnp.float32(-jnp.inf)
    big = jnp.float32(1e9)

    gmask = lane < N_GROUPS
    gl = jnp.where(gmask, logits, neg)
    gmax = jnp.max(gl, axis=-1, keepdims=True)
    g_sel = jnp.min(jnp.where(gl == gmax, lane, big), axis=-1, keepdims=True)
    g_p = 1.0 / jnp.sum(jnp.where(gmask, jnp.exp(logits - gmax), 0.0), axis=-1, keepdims=True)

    lo = ROUTER_COL0 + GROUP_SIZE * g_sel
    el = jnp.where((lane >= lo) & (lane < lo + GROUP_SIZE), logits, neg)
    m1 = jnp.max(el, axis=-1, keepdims=True)
    i1 = jnp.min(jnp.where(el == m1, lane, big), axis=-1, keepdims=True)
    el2 = jnp.where(lane == i1, neg, el)
    m2 = jnp.max(el2, axis=-1, keepdims=True)
    i2 = jnp.min(jnp.where(el2 == m2, lane, big), axis=-1, keepdims=True)
    r = jnp.exp(m2 - m1)
    c1 = g_p / (1.0 + r)
    c2 = g_p * r / (1.0 + r)

    oh1 = lane == i1
    oh2 = lane == i2
    ri = lax.broadcasted_iota(jnp.int32, (t, t), 0)
    ci = lax.broadcasted_iota(jnp.int32, (t, t), 1)
    tri = jnp.where(ci < ri, 1.0, 0.0).astype(BF16)
    oh1f = jnp.where(oh1, 1.0, 0.0)
    oh2f = jnp.where(oh2, 1.0, 0.0)
    pre1 = jnp.dot(tri, oh1f.astype(BF16), preferred_element_type=F32)
    pre2 = jnp.dot(tri, oh2f.astype(BF16), preferred_element_type=F32)
    tot1 = jnp.sum(oh1f, axis=0, keepdims=True)
    tot2 = jnp.sum(oh2f, axis=0, keepdims=True)
    cnt = cnt_acc[...]
    rank1 = jnp.sum(oh1f * (pre1 + cnt), axis=-1, keepdims=True)
    rank2 = jnp.sum(oh2f * (pre2 + tot1 + cnt), axis=-1, keepdims=True)
    cnt_new = cnt + tot1 + tot2
    cnt_acc[...] = cnt_new
    cnt_ref[...] = cnt_new

    meta = jnp.where(lane == 0, i1 - ROUTER_COL0, 0.0)
    meta = jnp.where(lane == 1, i2 - ROUTER_COL0, meta)
    meta = jnp.where(lane == 2, rank1, meta)
    meta = jnp.where(lane == 3, rank2, meta)
    meta = jnp.where(lane == 4, c1, meta)
    meta = jnp.where(lane == 5, c2, meta)
    meta_ref[...] = meta


def _route(x, g, w, b):
    n = x.shape[0]
    return pl.pallas_call(
        _route_kernel,
        grid=(n // ROUTE_T,),
        in_specs=[
            pl.BlockSpec((ROUTE_T, D_MODEL), lambda i: (i, 0)),
            pl.BlockSpec((1, D_MODEL), lambda i: (0, 0)),
            pl.BlockSpec((D_MODEL, LANES), lambda i: (0, 0)),
            pl.BlockSpec((1, LANES), lambda i: (0, 0)),
        ],
        out_specs=[
            pl.BlockSpec((ROUTE_T * N_COL, LANES), lambda i: (i, 0)),
            pl.BlockSpec((ROUTE_T, LANES), lambda i: (i, 0)),
            pl.BlockSpec((1, LANES), lambda i: (0, 0)),
        ],
        out_shape=[
            jax.ShapeDtypeStruct((n * N_COL, LANES), F32),
            jax.ShapeDtypeStruct((n, LANES), F32),
            jax.ShapeDtypeStruct((1, LANES), F32),
        ],
        scratch_shapes=[pltpu.VMEM((1, LANES), F32)],
        compiler_params=pltpu.CompilerParams(
            dimension_semantics=("arbitrary",), vmem_limit_bytes=VMEM_LIMIT),
        name="route",
    )(x, g, w, b)


def _zero_pad_slots(pad_start_ref, pad_len_ref, n_used_ref, zeros_ref, out_ref, sem):
    half = EXPERT_TM // 2
    sizes = [1 << b for b in reversed(range(EXPERT_TM.bit_length() - 1))]
    n_tiles = out_ref.shape[0] // (EXPERT_TM * N_COL)

    def zero_copy(at, size):
        return pltpu.make_async_copy(
            zeros_ref.at[pl.ds(0, size * N_COL), :],
            out_ref.at[pl.ds(pl.multiple_of(at * N_COL, N_COL), size * N_COL), :], sem)

    def pad_copies(e, act):
        at = pad_start_ref[e]
        for size in sizes:
            take = (pad_len_ref[e] & size) != 0

            @pl.when(take)
            def _(at=at, size=size):
                act(zero_copy(at, size))

            at = at + jnp.where(take, size, 0)

    def tail_copies(j, act):
        act(zero_copy(j * EXPERT_TM, half))
        act(zero_copy(j * EXPERT_TM + half, half))

    def loops(act):
        def pad_body(e, carry):
            pad_copies(e, act)
            return carry

        def tail_body(j, carry):
            tail_copies(j, act)
            return carry

        lax.fori_loop(0, N_EXPERTS, pad_body, 0)
        lax.fori_loop(n_used_ref[0], n_tiles, tail_body, 0)

    loops(lambda cp: cp.start())
    loops(lambda cp: cp.wait())


def _dispatch_kernel(pos_ref, pad_start_ref, pad_len_ref, n_used_ref, z_ref, out_ref, zeros_ref,
                     sem, pad_sem):
    @pl.when(pl.program_id(0) == 0)
    def _():
        zeros_ref[...] = jnp.zeros_like(zeros_ref)
        _zero_pad_slots(pad_start_ref, pad_len_ref, n_used_ref, zeros_ref, out_ref, pad_sem)

    def start(g, carry):
        for u in range(ROW_UNROLL):
            r = g * ROW_UNROLL + u
            for k in range(2):
                pltpu.make_async_copy(
                    _token_tile(z_ref, r), _token_tile(out_ref, pos_ref[2 * r + k]), sem).start()
        return carry

    lax.fori_loop(0, DISPATCH_T // ROW_UNROLL, start, 0)
    for _ in range(2):
        pltpu.make_async_copy(z_ref, out_ref.at[pl.ds(0, DISPATCH_T * N_COL), :], sem).wait()


def _dispatch(pos_flat, pad_start, pad_len, n_used, z, n_slots):
    n = z.shape[0] // N_COL
    return pl.pallas_call(
        _dispatch_kernel,
        grid=(n // DISPATCH_T,),
        in_specs=[
            pl.BlockSpec((2 * DISPATCH_T,), lambda i: (i,), memory_space=pltpu.SMEM),
            pl.BlockSpec(memory_space=pltpu.SMEM),
            pl.BlockSpec(memory_space=pltpu.SMEM),
            pl.BlockSpec(memory_space=pltpu.SMEM),
            pl.BlockSpec((DISPATCH_T * N_COL, LANES), lambda i: (i, 0)),
        ],
        out_specs=pl.BlockSpec(memory_space=pl.ANY),
        out_shape=jax.ShapeDtypeStruct((n_slots * N_COL, LANES), F32),
        scratch_shapes=[
            pltpu.VMEM((EXPERT_TM // 2 * N_COL, LANES), F32),
            pltpu.SemaphoreType.DMA,
            pltpu.SemaphoreType.DMA,
        ],
        compiler_params=pltpu.CompilerParams(
            dimension_semantics=("arbitrary",), vmem_limit_bytes=VMEM_LIMIT),
        name="dispatch",
    )(pos_flat, pad_start, pad_len, n_used, z)


def _experts_kernel(te_ref, nu_ref, z_ref, wgu_ref, wd_ref, y_ref, wgu_bf, wd_bf):
    j = pl.program_id(0)
    used = j < nu_ref[0]
    new_expert = (j == 0) | (te_ref[j] != te_ref[jnp.maximum(j - 1, 0)])

    @pl.when(used & new_expert)
    def _():
        wgu_bf[...] = wgu_ref[0, 0].astype(BF16)
        wd_bf[...] = wd_ref[0, 0].astype(BF16)

    @pl.when(used)
    def _():
        z = _load_tiled(z_ref, EXPERT_TM).astype(BF16)
        gu = jnp.dot(z, wgu_bf[...], preferred_element_type=F32)
        gate = gu[:, :D_FF]
        up = gu[:, D_FF:]
        hmid = (gate * _sigmoid(gate) * up).astype(BF16)
        _store_tiled(y_ref, jnp.dot(hmid, wd_bf[...], preferred_element_type=F32))

    @pl.when(jnp.logical_not(used))
    def _():
        y_ref[...] = jnp.zeros_like(y_ref)


def _experts(tile_expert, n_used, zs, wgu_all, wd_all, layer):
    n_slots = zs.shape[0] // N_COL
    n_tiles = n_slots // EXPERT_TM
    last = lambda j, nu: jnp.maximum(jnp.minimum(j, nu[0] - 1), 0)
    return pl.pallas_call(
        _experts_kernel,
        grid_spec=pltpu.PrefetchScalarGridSpec(
            num_scalar_prefetch=2,
            grid=(n_tiles,),
            in_specs=[
                pl.BlockSpec((EXPERT_TM * N_COL, LANES), lambda j, te, nu: (last(j, nu), 0)),
                pl.BlockSpec((1, 1, D_MODEL, 2 * D_FF), lambda j, te, nu: (layer, te[j], 0, 0)),
                pl.BlockSpec((1, 1, D_FF, D_MODEL), lambda j, te, nu: (layer, te[j], 0, 0)),
            ],
            out_specs=pl.BlockSpec((EXPERT_TM * N_COL, LANES), lambda j, te, nu: (j, 0)),
            scratch_shapes=[pltpu.VMEM((D_MODEL, 2 * D_FF), BF16), pltpu.VMEM((D_FF, D_MODEL), BF16)],
        ),
        out_shape=jax.ShapeDtypeStruct((n_slots * N_COL, LANES), F32),
        compiler_params=pltpu.CompilerParams(
            dimension_semantics=("arbitrary",), vmem_limit_bytes=VMEM_LIMIT),
        name="experts",
    )(tile_expert, n_used, zs, wgu_all, wd_all)


def _combine_kernel(pos_ref, pos_next_ref, x_ref, meta_ref, g_ref, y_ref, o_ref, buf0, buf1, sems,
                    *, final_norm):
    bufs = (buf0, buf1)
    i = pl.program_id(0)
    slot = i % 2

    def gather(pos, into):
        def start(g, carry):
            for u in range(ROW_UNROLL):
                r = g * ROW_UNROLL + u
                for k in range(2):
                    pltpu.make_async_copy(_token_tile(y_ref, pos[2 * r + k]),
                                          _token_tile(bufs[k].at[into], r), sems.at[into]).start()
            return carry

        lax.fori_loop(0, COMBINE_T // ROW_UNROLL, start, 0)

    @pl.when(i == 0)
    def _():
        gather(pos_ref, 0)

    @pl.when(i + 1 < pl.num_programs(0))
    def _():
        gather(pos_next_ref, 1 - slot)

    for k in range(2):
        pltpu.make_async_copy(y_ref.at[pl.ds(0, COMBINE_T * N_COL), :], bufs[k].at[slot],
                              sems.at[slot]).wait()
    meta = meta_ref[...]
    y0 = _load_tiled(buf0.at[slot], COMBINE_T)
    y1 = _load_tiled(buf1.at[slot], COMBINE_T)
    out = x_ref[...] + meta[:, 4:5] * y0 + meta[:, 5:6] * y1
    if final_norm:
        out = _rms(out, g_ref[...])
    o_ref[...] = out


def _combine(pos_flat, x, meta, g, y, final_norm):
    n = x.shape[0]
    n_steps = n // COMBINE_T
    return pl.pallas_call(
        functools.partial(_combine_kernel, final_norm=final_norm),
        grid=(n_steps,),
        in_specs=[
            pl.BlockSpec((2 * COMBINE_T,), lambda i: (i,), memory_space=pltpu.SMEM),
            pl.BlockSpec((2 * COMBINE_T,), lambda i: (jnp.minimum(i + 1, n_steps - 1),),
                         memory_space=pltpu.SMEM),
            pl.BlockSpec((COMBINE_T, D_MODEL), lambda i: (i, 0)),
            pl.BlockSpec((COMBINE_T, LANES), lambda i: (i, 0)),
            pl.BlockSpec((1, D_MODEL), lambda i: (0, 0)),
            pl.BlockSpec(memory_space=pl.ANY),
        ],
        out_specs=pl.BlockSpec((COMBINE_T, D_MODEL), lambda i: (i, 0)),
        out_shape=jax.ShapeDtypeStruct((n, D_MODEL), F32),
        scratch_shapes=[
            pltpu.VMEM((2, COMBINE_T * N_COL, LANES), F32),
            pltpu.VMEM((2, COMBINE_T * N_COL, LANES), F32),
            pltpu.SemaphoreType.DMA((2,)),
        ],
        compiler_params=pltpu.CompilerParams(
            dimension_semantics=("arbitrary",), vmem_limit_bytes=VMEM_LIMIT),
        name="combine",
    )(pos_flat, pos_flat, x, meta, g, y)


def _slot_plan(meta, counts, n_tiles):
    cnt = counts[0, ROUTER_COL0:ROUTER_COL0 + N_EXPERTS].astype(jnp.int32)
    padded = (cnt + EXPERT_TM - 1) // EXPERT_TM * EXPERT_TM
    ends = jnp.cumsum(padded)
    offs = ends - padded
    e = meta[:, 0:2].astype(jnp.int32)
    rank = meta[:, 2:4].astype(jnp.int32)
    onehot = e[..., None] == jnp.arange(N_EXPERTS, dtype=jnp.int32)
    pos = jnp.sum(jnp.where(onehot, offs, 0), axis=-1) + rank
    tile_ends = ends // EXPERT_TM
    tile_ids = jnp.arange(n_tiles, dtype=jnp.int32)
    tile_expert = jnp.sum(tile_ids[:, None] >= tile_ends[None, :], axis=1).astype(jnp.int32)
    tile_expert = jnp.minimum(tile_expert, N_EXPERTS - 1)
    n_used = tile_ends[-1:].astype(jnp.int32)
    return pos.reshape(-1), tile_expert, n_used, offs + cnt, padded - cnt


def _col_major(w):
    return w.reshape(w.shape[0], N_COL, LANES).transpose(1, 0, 2)


def kernel(x, mem, mix_norm_g, mem_norm_g, w_in, b_gate, conv_a_w, w_out_a, b_glu, conv_b_w,
           conv_b_b, ln_b_g, ln_b_b, w_out_b, b_out_b, w_kv, w_out_x, w_o, ffn_norm_g, w_group,
           b_group, w_router, b_router, w_gate_up, w_down, final_norm_g):
    batch, seq, d = x.shape
    depth = w_in.shape[0]
    n = batch * seq
    n_tiles = (2 * n) // EXPERT_TM + N_EXPERTS
    n_slots = n_tiles * EXPERT_TM
    row = lambda a: a.reshape(1, -1)
    xf = x.reshape(n, d)
    pad = LANES - N_GROUPS - N_EXPERTS
    for l in range(depth):
        k, v = _kv_proj(mem, row(mem_norm_g[l]), w_kv[l].astype(BF16))
        h = _proj_in(xf, row(mix_norm_g[l]), w_in, l)
        x1 = _mixer(h, xf, k, v, _col_major(conv_a_w[l]), _col_major(conv_b_w[l]), row(b_glu[l]),
                    _col_major(row(conv_b_b[l])), row(ln_b_g[l]), row(ln_b_b[l]), row(b_out_b[l]),
                    row(b_gate[l]), w_out_a[l].astype(BF16), w_out_b[l].astype(BF16),
                    w_out_x[l].astype(BF16), w_o[l].astype(BF16), batch)
        w_r = jnp.pad(jnp.concatenate([w_group[l], w_router[l]], axis=1), ((0, 0), (0, pad)))
        b_r = jnp.pad(jnp.concatenate([b_group[l], b_router[l]]), (0, pad)).reshape(1, LANES)
        z, meta, counts = _route(x1, row(ffn_norm_g[l]), w_r, b_r)
        pos, tile_expert, n_used, pad_start, pad_len = _slot_plan(meta, counts, n_tiles)
        zs = _dispatch(pos, pad_start, pad_len, n_used, z, n_slots)
        y = _experts(tile_expert, n_used, zs, w_gate_up, w_down, l)
        xf = _combine(pos, x1, meta, row(final_norm_g), y, final_norm=(l == depth - 1))
    return xf.reshape(batch, seq, d)
```

```python
import functools

import jax
import jax.numpy as jnp
from jax import lax
from jax.experimental import pallas as pl
from jax.experimental.pallas import tpu as pltpu

F32 = jnp.float32
BF16 = jnp.bfloat16

D_MODEL = 1024
N_HEADS = 4
HEAD_DIM = D_MODEL // N_HEADS
MEM_LEN = 256
A_TAPS = 3
B_TAPS = 31
N_GROUPS = 4
GROUP_SIZE = 8
N_EXPERTS = N_GROUPS * GROUP_SIZE
D_FF = D_MODEL // 2
N_SEG = 9
EPS = 1e-6

LANES = 128
SUBLANES = 8
N_COL = D_MODEL // LANES

PROJ_TM = 2048
PROJ_TN = 1024
MIX_T = 512
A_HIST = 8
B_HIST = 32
A_M = (MIX_T + A_HIST) // SUBLANES
B_M = (MIX_T + B_HIST) // SUBLANES
ROUTE_T = 512
DISPATCH_T = 512
EXPERT_TM = 512
COMBINE_T = 512
ROW_UNROLL = 8
ROUTER_COL0 = N_GROUPS

VMEM_LIMIT = 56 * 1024 * 1024


def _rms(x, g):
    return x * lax.rsqrt(jnp.mean(x * x, axis=-1, keepdims=True) + EPS) * g


def _sigmoid(x):
    return 1.0 / (1.0 + jnp.exp(-x))


def _cols(x):
    return [x[:, c * LANES:(c + 1) * LANES] for c in range(N_COL)]


def _token_tile(ref, r, rows=N_COL):
    return ref.at[pl.ds(pl.multiple_of(r * rows, rows), rows), :]


Z_ROWS = N_COL


def _load_tiled(ref, rows):
    return jnp.concatenate([ref[pl.ds(c, rows, stride=N_COL), :] for c in range(N_COL)], axis=-1)


def _store_tiled(ref, x):
    for c, part in enumerate(_cols(x)):
        ref[pl.ds(c, x.shape[0], stride=N_COL), :] = part


def _kv_kernel(mem_ref, g_ref, w_ref, k_ref, v_ref):
    mn = _rms(mem_ref[0], g_ref[...]).astype(BF16)
    kv = jnp.dot(mn, w_ref[...], preferred_element_type=F32)
    k_ref[0] = (kv[:, :D_MODEL] * (HEAD_DIM ** -0.5)).astype(BF16)
    v_ref[0] = kv[:, D_MODEL:].astype(BF16)


def _kv_proj(mem, g, w_kv):
    b = mem.shape[0]
    return pl.pallas_call(
        _kv_kernel,
        grid=(b,),
        in_specs=[
            pl.BlockSpec((1, MEM_LEN, D_MODEL), lambda i: (i, 0, 0)),
            pl.BlockSpec((1, D_MODEL), lambda i: (0, 0)),
            pl.BlockSpec((D_MODEL, 2 * D_MODEL), lambda i: (0, 0)),
        ],
        out_specs=[
            pl.BlockSpec((1, MEM_LEN, D_MODEL), lambda i: (i, 0, 0)),
            pl.BlockSpec((1, MEM_LEN, D_MODEL), lambda i: (i, 0, 0)),
        ],
        out_shape=[jax.ShapeDtypeStruct((b, MEM_LEN, D_MODEL), BF16)] * 2,
        compiler_params=pltpu.CompilerParams(vmem_limit_bytes=VMEM_LIMIT),
        name="kv_proj",
    )(mem, g, w_kv)


def _proj_in_kernel(x_ref, g_ref, w_ref, o_ref, xn_ref):
    @pl.when(pl.program_id(1) == 0)
    def _():
        xn_ref[...] = _rms(x_ref[...], g_ref[...]).astype(BF16)

    o_ref[...] = jnp.dot(xn_ref[...], w_ref[0].astype(BF16),
                         preferred_element_type=F32).astype(BF16)


def _proj_in(x, g, w_all, layer):
    n = x.shape[0]
    n_in = w_all.shape[2]
    return pl.pallas_call(
        _proj_in_kernel,
        grid=(n // PROJ_TM, n_in // PROJ_TN),
        in_specs=[
            pl.BlockSpec((PROJ_TM, D_MODEL), lambda i, j: (i, 0)),
            pl.BlockSpec((1, D_MODEL), lambda i, j: (0, 0)),
            pl.BlockSpec((1, D_MODEL, PROJ_TN), lambda i, j: (layer, 0, j)),
        ],
        out_specs=pl.BlockSpec((PROJ_TM, PROJ_TN), lambda i, j: (i, j)),
        out_shape=jax.ShapeDtypeStruct((n, n_in), BF16),
        scratch_shapes=[pltpu.VMEM((PROJ_TM, D_MODEL), BF16)],
        compiler_params=pltpu.CompilerParams(
            dimension_semantics=("arbitrary", "arbitrary"), vmem_limit_bytes=VMEM_LIMIT),
        name="proj_in",
    )(x, g, w_all)


def _causal_conv(src, wrap, dst, taps, bias, col, m, n_taps):
    first_wrap = m - (n_taps - 1)
    view = lambda ref, j: ref[col, pl.ds(j, SUBLANES, stride=m), :]
    for j in range(first_wrap, m):
        wrap[col, pl.ds((j - first_wrap) * SUBLANES, SUBLANES), :] = pltpu.roll(view(src, j), 1, 0)
    w = [jnp.broadcast_to(taps[col, k:k + 1, :], (SUBLANES, LANES)) for k in range(n_taps)]
    for i in range(m):
        acc = None if bias is None else jnp.broadcast_to(bias[col], (SUBLANES, LANES))
        for s in range(n_taps):
            j = i - s
            if j >= 0:
                x = view(src, j)
            else:
                x = wrap[col, pl.ds((j + m - first_wrap) * SUBLANES, SUBLANES), :]
            term = w[n_taps - 1 - s] * x
            acc = term if acc is None else acc + term
        dst[col, pl.ds(i, SUBLANES, stride=m), :] = acc


def _mixer_kernel(aval, agb, agc, gluv, glug, q_ref, g0, g1, g2, x_ref, k_ref, v_ref,
                  caw, cbw, bglu, cbb, lng, lnb, bob, bgate, woa, wob, wox, wo,
                  o_ref, a_src, a_wrap, a_dst, b_src, b_wrap, b_dst):
    t = MIX_T

    @pl.when(pl.program_id(1) == 0)
    def _():
        a_src[:, 0:A_HIST, :] = jnp.zeros((N_COL, A_HIST, LANES), F32)
        b_src[:, 0:B_HIST, :] = jnp.zeros((N_COL, B_HIST, LANES), F32)

    cv = agc[...].astype(F32) * aval[...].astype(F32)
    for c, part in enumerate(_cols(cv)):
        a_src[c, A_HIST:A_HIST + t, :] = part
    for c in range(N_COL):
        _causal_conv(a_src, a_wrap, a_dst, caw, None, c, A_M, A_TAPS)
    conv_a = jnp.concatenate([a_dst[c, A_HIST:A_HIST + t, :] for c in range(N_COL)], axis=-1)
    a_src[:, 0:A_HIST, :] = a_src[:, t:t + A_HIST, :]
    ya_in = (agb[...].astype(F32) * conv_a).astype(BF16)
    y_a = jnp.dot(ya_in, woa[...], preferred_element_type=F32)
    merged = _sigmoid(g0[...].astype(F32) + bgate[:, 0:D_MODEL]) * y_a

    gv = gluv[...].astype(F32) + bglu[:, 0:D_MODEL]
    gg = glug[...].astype(F32) + bglu[:, D_MODEL:2 * D_MODEL]
    for c, part in enumerate(_cols(gv * _sigmoid(gg))):
        b_src[c, B_HIST:B_HIST + t, :] = part

    def conv_col(c, carry):
        _causal_conv(b_src, b_wrap, b_dst, cbw, cbb, c, B_M, B_TAPS)
        return carry

    lax.fori_loop(0, N_COL, conv_col, 0)
    conv_b = jnp.concatenate([b_dst[c, B_HIST:B_HIST + t, :] for c in range(N_COL)], axis=-1)
    b_src[:, 0:B_HIST, :] = b_src[:, t:t + B_HIST, :]
    mu = jnp.mean(conv_b, axis=-1, keepdims=True)
    xc = conv_b - mu
    var = jnp.mean(xc * xc, axis=-1, keepdims=True)
    ln = xc * lax.rsqrt(var + EPS) * lng[...] + lnb[...]
    y_b = jnp.dot((ln * _sigmoid(ln)).astype(BF16), wob[...], preferred_element_type=F32) + bob[...]
    merged += _sigmoid(g1[...].astype(F32) + bgate[:, D_MODEL:2 * D_MODEL]) * y_b

    heads = []
    for h in range(N_HEADS):
        sl = slice(h * HEAD_DIM, (h + 1) * HEAD_DIM)
        s = lax.dot_general(q_ref[:, sl], k_ref[0, :, sl], (((1,), (1,)), ((), ())),
                            preferred_element_type=F32)
        p = jnp.exp(s - jnp.max(s, axis=-1, keepdims=True))
        denom = jnp.sum(p, axis=-1, keepdims=True)
        o = jnp.dot(p.astype(BF16), v_ref[0, :, sl], preferred_element_type=F32)
        heads.append((o / denom).astype(BF16))
    attn = jnp.concatenate(heads, axis=-1)
    y_x = jnp.dot(attn, wox[...], preferred_element_type=F32)
    merged += _sigmoid(g2[...].astype(F32) + bgate[:, 2 * D_MODEL:3 * D_MODEL]) * y_x

    o_ref[...] = x_ref[...] + jnp.dot(merged.astype(BF16), wo[...], preferred_element_type=F32)


def _mixer(h, x, k, v, caw, cbw, bglu, cbb, lng, lnb, bob, bgate, woa, wob, wox, wo, batch):
    n = x.shape[0]
    n_s = n // batch // MIX_T
    row = lambda b, s: b * n_s + s
    seg_specs = [pl.BlockSpec((MIX_T, D_MODEL), functools.partial(lambda b, s, c: (row(b, s), c), c=c))
                 for c in range(N_SEG)]
    const2 = lambda shape: pl.BlockSpec(shape, lambda b, s: (0, 0))
    const3 = lambda shape: pl.BlockSpec(shape, lambda b, s: (0, 0, 0))
    kv_spec = pl.BlockSpec((1, MEM_LEN, D_MODEL), lambda b, s: (b, 0, 0))
    w_spec = pl.BlockSpec((D_MODEL, D_MODEL), lambda b, s: (0, 0), pipeline_mode=pl.Buffered(1))
    col_buf = lambda rows: pltpu.VMEM((N_COL, rows, LANES), F32)
    return pl.pallas_call(
        _mixer_kernel,
        grid=(batch, n_s),
        in_specs=seg_specs + [
            pl.BlockSpec((MIX_T, D_MODEL), lambda b, s: (row(b, s), 0)),
            kv_spec, kv_spec,
            const3((N_COL, A_TAPS, LANES)), const3((N_COL, B_TAPS, LANES)),
            const2((1, 2 * D_MODEL)), const3((N_COL, 1, LANES)), const2((1, D_MODEL)),
            const2((1, D_MODEL)), const2((1, D_MODEL)), const2((1, 3 * D_MODEL)),
            w_spec, w_spec, w_spec, w_spec,
        ],
        out_specs=pl.BlockSpec((MIX_T, D_MODEL), lambda b, s: (row(b, s), 0)),
        out_shape=jax.ShapeDtypeStruct((n, D_MODEL), F32),
        scratch_shapes=[
            col_buf(SUBLANES * A_M), col_buf(SUBLANES * (A_TAPS - 1)), col_buf(SUBLANES * A_M),
            col_buf(SUBLANES * B_M), col_buf(SUBLANES * (B_TAPS - 1)), col_buf(SUBLANES * B_M),
        ],
        compiler_params=pltpu.CompilerParams(
            dimension_semantics=("arbitrary", "arbitrary"), vmem_limit_bytes=VMEM_LIMIT),
        name="mixer",
    )(*([h] * N_SEG), x, k, v, caw, cbw, bglu, cbb, lng, lnb, bob, bgate, woa, wob, wox, wo)


def _route_kernel(x_ref, g_ref, w_ref, b_ref, z_ref, meta_ref, cnt_ref, cnt_acc):
    t = ROUTE_T

    @pl.when(pl.program_id(0) == 0)
    def _():
        cnt_acc[...] = jnp.zeros_like(cnt_acc)

    z = _rms(x_ref[...], g_ref[...])
    _store_tiled(z_ref, z)
    logits = jnp.dot(z, w_ref[...], preferred_element_type=F32,
                     precision=lax.Precision.HIGHEST) + b_ref[...]
    lane = lax.broadcasted_iota(jnp.int32, (t, LANES), 1).astype(F32)
    neg = jnp.float32(-jnp.inf)
    big = jnp.float32(1e9)

    gmask = lane < N_GROUPS
    gl = jnp.where(gmask, logits, neg)
    gmax = jnp.max(gl, axis=-1, keepdims=True)
    g_sel = jnp.min(jnp.where(gl == gmax, lane, big), axis=-1, keepdims=True)
    g_p = 1.0 / jnp.sum(jnp.where(gmask, jnp.exp(logits - gmax), 0.0), axis=-1, keepdims=True)

    lo = ROUTER_COL0 + GROUP_SIZE * g_sel
    el = jnp.where((lane >= lo) & (lane < lo + GROUP_SIZE), logits, neg)
    m1 = jnp.max(el, axis=-1, keepdims=True)
    i1 = jnp.min(jnp.where(el == m1, lane, big), axis=-1, keepdims=True)
    el2 = jnp.where(lane == i1, neg, el)
    m2 = jnp.max(el2, axis=-1, keepdims=True)
    i2 = jnp.min(jnp.where(el2 == m2, lane, big), axis=-1, keepdims=True)
    r = jnp.exp(m2 - m1)
    c1 = g_p / (1.0 + r)
    c2 = g_p * r / (1.0 + r)

    oh1 = lane == i1
    oh2 = lane == i2
    ri = lax.broadcasted_iota(jnp.int32, (t, t), 0)
    ci = lax.broadcasted_iota(jnp.int32, (t, t), 1)
    tri = jnp.where(ci < ri, 1.0, 0.0).astype(BF16)
    oh1f = jnp.where(oh1, 1.0, 0.0)
    oh2f = jnp.where(oh2, 1.0, 0.0)
    pre1 = jnp.dot(tri, oh1f.astype(BF16), preferred_element_type=F32)
    pre2 = jnp.dot(tri, oh2f.astype(BF16), preferred_element_type=F32)
    tot1 = jnp.sum(oh1f, axis=0, keepdims=True)
    tot2 = jnp.sum(oh2f, axis=0, keepdims=True)
    cnt = cnt_acc[...]
    rank1 = jnp.sum(oh1f * (pre1 + cnt), axis=-1, keepdims=True)
    rank2 = jnp.sum(oh2f * (pre2 + tot1 + cnt), axis=-1, keepdims=True)
    cnt_new = cnt + tot1 + tot2
    cnt_acc[...] = cnt_new
    cnt_ref[...] = cnt_new

    meta = jnp.where(lane == 0, i1 - ROUTER_COL0, 0.0)
    meta = jnp.where(lane == 1, i2 - ROUTER_COL0, meta)
    meta = jnp.where(lane == 2, rank1, meta)
    meta = jnp.where(lane == 3, rank2, meta)
    meta = jnp.where(lane == 4, c1, meta)
    meta = jnp.where(lane == 5, c2, meta)
    meta_ref[...] = meta


def _route(x, g, w, b):
    n = x.shape[0]
    return pl.pallas_call(
        _route_kernel,
        grid=(n // ROUTE_T,),
        in_specs=[
            pl.BlockSpec((ROUTE_T, D_MODEL), lambda i: (i, 0)),
            pl.BlockSpec((1, D_MODEL), lambda i: (0, 0)),
            pl.BlockSpec((D_MODEL, LANES), lambda i: (0, 0)),
            pl.BlockSpec((1, LANES), lambda i: (0, 0)),
        ],
        out_specs=[
            pl.BlockSpec((ROUTE_T * Z_ROWS, LANES), lambda i: (i, 0)),
            pl.BlockSpec((ROUTE_T, LANES), lambda i: (i, 0)),
            pl.BlockSpec((1, LANES), lambda i: (0, 0)),
        ],
        out_shape=[
            jax.ShapeDtypeStruct((n * Z_ROWS, LANES), F32),
            jax.ShapeDtypeStruct((n, LANES), F32),
            jax.ShapeDtypeStruct((1, LANES), F32),
        ],
        scratch_shapes=[pltpu.VMEM((1, LANES), F32)],
        compiler_params=pltpu.CompilerParams(
            dimension_semantics=("arbitrary",), vmem_limit_bytes=VMEM_LIMIT),
        name="route",
    )(x, g, w, b)


def _zero_pad_slots(pad_start_ref, pad_len_ref, n_used_ref, zeros_ref, out_ref, sem):
    half = EXPERT_TM // 2
    sizes = [1 << b for b in reversed(range(EXPERT_TM.bit_length() - 1))]
    n_tiles = out_ref.shape[0] // (EXPERT_TM * Z_ROWS)

    def zero_copy(at, size):
        return pltpu.make_async_copy(
            zeros_ref.at[pl.ds(0, size * Z_ROWS), :],
            out_ref.at[pl.ds(pl.multiple_of(at * Z_ROWS, Z_ROWS), size * Z_ROWS), :], sem)

    def pad_copies(e, act):
        at = pad_start_ref[e]
        for size in sizes:
            take = (pad_len_ref[e] & size) != 0

            @pl.when(take)
            def _(at=at, size=size):
                act(zero_copy(at, size))

            at = at + jnp.where(take, size, 0)

    def tail_copies(j, act):
        act(zero_copy(j * EXPERT_TM, half))
        act(zero_copy(j * EXPERT_TM + half, half))

    def loops(act):
        def pad_body(e, carry):
            pad_copies(e, act)
            return carry

        def tail_body(j, carry):
            tail_copies(j, act)
            return carry

        lax.fori_loop(0, N_EXPERTS, pad_body, 0)
        lax.fori_loop(n_used_ref[0], n_tiles, tail_body, 0)

    loops(lambda cp: cp.start())
    loops(lambda cp: cp.wait())


def _dispatch_kernel(pos_ref, pad_start_ref, pad_len_ref, n_used_ref, z_ref, out_ref, zeros_ref,
                     sem, pad_sem):
    @pl.when(pl.program_id(0) == 0)
    def _():
        zeros_ref[...] = jnp.zeros_like(zeros_ref)
        _zero_pad_slots(pad_start_ref, pad_len_ref, n_used_ref, zeros_ref, out_ref, pad_sem)

    def start(g, carry):
        for u in range(ROW_UNROLL):
            r = g * ROW_UNROLL + u
            for k in range(2):
                pltpu.make_async_copy(
                    _token_tile(z_ref, r, Z_ROWS),
                    _token_tile(out_ref, pos_ref[2 * r + k], Z_ROWS), sem).start(priority=k)
        return carry

    lax.fori_loop(0, DISPATCH_T // ROW_UNROLL, start, 0)
    for _ in range(2):
        pltpu.make_async_copy(z_ref, out_ref.at[pl.ds(0, DISPATCH_T * Z_ROWS), :], sem).wait()


def _dispatch(pos_flat, pad_start, pad_len, n_used, z, n_slots):
    n = z.shape[0] // Z_ROWS
    return pl.pallas_call(
        _dispatch_kernel,
        grid=(n // DISPATCH_T,),
        in_specs=[
            pl.BlockSpec((2 * DISPATCH_T,), lambda i: (i,), memory_space=pltpu.SMEM),
            pl.BlockSpec(memory_space=pltpu.SMEM),
            pl.BlockSpec(memory_space=pltpu.SMEM),
            pl.BlockSpec(memory_space=pltpu.SMEM),
            pl.BlockSpec((DISPATCH_T * Z_ROWS, LANES), lambda i: (i, 0)),
        ],
        out_specs=pl.BlockSpec(memory_space=pl.ANY),
        out_shape=jax.ShapeDtypeStruct((n_slots * Z_ROWS, LANES), F32),
        scratch_shapes=[
            pltpu.VMEM((EXPERT_TM // 2 * Z_ROWS, LANES), F32),
            pltpu.SemaphoreType.DMA,
            pltpu.SemaphoreType.DMA,
        ],
        compiler_params=pltpu.CompilerParams(
            dimension_semantics=("arbitrary",), vmem_limit_bytes=VMEM_LIMIT),
        name="dispatch",
    )(pos_flat, pad_start, pad_len, n_used, z)


def _experts_kernel(te_ref, nu_ref, z_ref, wgu_ref, wd_ref, y_ref, wgu_bf, wd_bf):
    j = pl.program_id(0)
    used = j < nu_ref[0]
    new_expert = (j == 0) | (te_ref[j] != te_ref[jnp.maximum(j - 1, 0)])

    @pl.when(used & new_expert)
    def _():
        wgu_bf[...] = wgu_ref[0, 0].astype(BF16)
        wd_bf[...] = wd_ref[0, 0].astype(BF16)

    @pl.when(used)
    def _():
        z = _load_tiled(z_ref, EXPERT_TM).astype(BF16)
        gu = jnp.dot(z, wgu_bf[...], preferred_element_type=F32)
        gate = gu[:, :D_FF]
        up = gu[:, D_FF:]
        hmid = (gate * _sigmoid(gate) * up).astype(BF16)
        _store_tiled(y_ref, jnp.dot(hmid, wd_bf[...], preferred_element_type=F32))

    @pl.when(jnp.logical_not(used))
    def _():
        y_ref[...] = jnp.zeros_like(y_ref)


def _experts(tile_expert, n_used, zs, wgu_all, wd_all, layer):
    n_slots = zs.shape[0] // Z_ROWS
    n_tiles = n_slots // EXPERT_TM
    last = lambda j, nu: jnp.maximum(jnp.minimum(j, nu[0] - 1), 0)
    return pl.pallas_call(
        _experts_kernel,
        grid_spec=pltpu.PrefetchScalarGridSpec(
            num_scalar_prefetch=2,
            grid=(n_tiles,),
            in_specs=[
                pl.BlockSpec((EXPERT_TM * Z_ROWS, LANES), lambda j, te, nu: (last(j, nu), 0)),
                pl.BlockSpec((1, 1, D_MODEL, 2 * D_FF), lambda j, te, nu: (layer, te[j], 0, 0)),
                pl.BlockSpec((1, 1, D_FF, D_MODEL), lambda j, te, nu: (layer, te[j], 0, 0)),
            ],
            out_specs=pl.BlockSpec((EXPERT_TM * N_COL, LANES), lambda j, te, nu: (j, 0)),
            scratch_shapes=[pltpu.VMEM((D_MODEL, 2 * D_FF), BF16), pltpu.VMEM((D_FF, D_MODEL), BF16)],
        ),
        out_shape=jax.ShapeDtypeStruct((n_slots * N_COL, LANES), F32),
        compiler_params=pltpu.CompilerParams(
            dimension_semantics=("arbitrary",), vmem_limit_bytes=VMEM_LIMIT),
        name="experts",
    )(tile_expert, n_used, zs, wgu_all, wd_all)


def _combine_kernel(pos_ref, pos_next_ref, x_ref, meta_ref, g_ref, y_ref, o_ref, buf0, buf1, sems,
                    *, final_norm):
    bufs = (buf0, buf1)
    i = pl.program_id(0)
    slot = i % 2

    def gather(pos, into):
        def start(g, carry):
            for u in range(ROW_UNROLL):
                r = g * ROW_UNROLL + u
                for k in range(2):
                    pltpu.make_async_copy(
                        _token_tile(y_ref, pos[2 * r + k]), _token_tile(bufs[k].at[into], r),
                        sems.at[into]).start(priority=k)
            return carry

        lax.fori_loop(0, COMBINE_T // ROW_UNROLL, start, 0)

    @pl.when(i == 0)
    def _():
        gather(pos_ref, 0)

    @pl.when(i + 1 < pl.num_programs(0))
    def _():
        gather(pos_next_ref, 1 - slot)

    for k in range(2):
        pltpu.make_async_copy(y_ref.at[pl.ds(0, COMBINE_T * N_COL), :], bufs[k].at[slot],
                              sems.at[slot]).wait()
    meta = meta_ref[...]
    y0 = _load_tiled(buf0.at[slot], COMBINE_T)
    y1 = _load_tiled(buf1.at[slot], COMBINE_T)
    out = x_ref[...] + meta[:, 4:5] * y0 + meta[:, 5:6] * y1
    if final_norm:
        out = _rms(out, g_ref[...])
    o_ref[...] = out


def _combine(pos_flat, x, meta, g, y, final_norm):
    n = x.shape[0]
    n_steps = n // COMBINE_T
    return pl.pallas_call(
        functools.partial(_combine_kernel, final_norm=final_norm),
        grid=(n_steps,),
        in_specs=[
            pl.BlockSpec((2 * COMBINE_T,), lambda i: (i,), memory_space=pltpu.SMEM),
            pl.BlockSpec((2 * COMBINE_T,), lambda i: (jnp.minimum(i + 1, n_steps - 1),),
                         memory_space=pltpu.SMEM),
            pl.BlockSpec((COMBINE_T, D_MODEL), lambda i: (i, 0)),
            pl.BlockSpec((COMBINE_T, LANES), lambda i: (i, 0)),
            pl.BlockSpec((1, D_MODEL), lambda i: (0, 0)),
            pl.BlockSpec(memory_space=pl.ANY),
        ],
        out_specs=pl.BlockSpec((COMBINE_T, D_MODEL), lambda i: (i, 0)),
        out_shape=jax.ShapeDtypeStruct((n, D_MODEL), F32),
        scratch_shapes=[
            pltpu.VMEM((2, COMBINE_T * N_COL, LANES), F32),
            pltpu.VMEM((2, COMBINE_T * N_COL, LANES), F32),
            pltpu.SemaphoreType.DMA((2,)),
        ],
        compiler_params=pltpu.CompilerParams(
            dimension_semantics=("arbitrary",), vmem_limit_bytes=VMEM_LIMIT),
        name="combine",
    )(pos_flat, pos_flat, x, meta, g, y)


def _slot_plan(meta, counts, n_tiles):
    cnt = counts[0, ROUTER_COL0:ROUTER_COL0 + N_EXPERTS].astype(jnp.int32)
    padded = (cnt + EXPERT_TM - 1) // EXPERT_TM * EXPERT_TM
    ends = jnp.cumsum(padded)
    offs = ends - padded
    e = meta[:, 0:2].astype(jnp.int32)
    rank = meta[:, 2:4].astype(jnp.int32)
    onehot = e[..., None] == jnp.arange(N_EXPERTS, dtype=jnp.int32)
    pos = jnp.sum(jnp.where(onehot, offs, 0), axis=-1) + rank
    tile_ends = ends // EXPERT_TM
    tile_ids = jnp.arange(n_tiles, dtype=jnp.int32)
    tile_expert = jnp.sum(tile_ids[:, None] >= tile_ends[None, :], axis=1).astype(jnp.int32)
    tile_expert = jnp.minimum(tile_expert, N_EXPERTS - 1)
    n_used = tile_ends[-1:].astype(jnp.int32)
    return pos.reshape(-1), tile_expert, n_used, offs + cnt, padded - cnt


def _col_major(w):
    return w.reshape(w.shape[0], N_COL, LANES).transpose(1, 0, 2)


def kernel(x, mem, mix_norm_g, mem_norm_g, w_in, b_gate, conv_a_w, w_out_a, b_glu, conv_b_w,
           conv_b_b, ln_b_g, ln_b_b, w_out_b, b_out_b, w_kv, w_out_x, w_o, ffn_norm_g, w_group,
           b_group, w_router, b_router, w_gate_up, w_down, final_norm_g):
    batch, seq, d = x.shape
    depth = w_in.shape[0]
    n = batch * seq
    n_tiles = (2 * n) // EXPERT_TM + N_EXPERTS
    n_slots = n_tiles * EXPERT_TM
    row = lambda a: a.reshape(1, -1)
    xf = x.reshape(n, d)
    pad = LANES - N_GROUPS - N_EXPERTS
    for l in range(depth):
        k, v = _kv_proj(mem, row(mem_norm_g[l]), w_kv[l].astype(BF16))
        h = _proj_in(xf, row(mix_norm_g[l]), w_in, l)
        x1 = _mixer(h, xf, k, v, _col_major(conv_a_w[l]), _col_major(conv_b_w[l]), row(b_glu[l]),
                    _col_major(row(conv_b_b[l])), row(ln_b_g[l]), row(ln_b_b[l]), row(b_out_b[l]),
                    row(b_gate[l]), w_out_a[l].astype(BF16), w_out_b[l].astype(BF16),
                    w_out_x[l].astype(BF16), w_o[l].astype(BF16), batch)
        w_r = jnp.pad(jnp.concatenate([w_group[l], w_router[l]], axis=1), ((0, 0), (0, pad)))
        b_r = jnp.pad(jnp.concatenate([b_group[l], b_router[l]]), (0, pad)).reshape(1, LANES)
        z, meta, counts = _route(x1, row(ffn_norm_g[l]), w_r, b_r)
        pos, tile_expert, n_used, pad_start, pad_len = _slot_plan(meta, counts, n_tiles)
        zs = _dispatch(pos, pad_start, pad_len, n_used, z, n_slots)
        y = _experts(tile_expert, n_used, zs, w_gate_up, w_down, l)
        xf = _combine(pos, x1, meta, row(final_norm_g), y, final_norm=(l == depth - 1))
    return xf.reshape(batch, seq, d)
```

```python
import functools

import jax
import jax.numpy as jnp
from jax import lax
from jax.experimental import pallas as pl
from jax.experimental.pallas import tpu as pltpu

F32 = jnp.float32
BF16 = jnp.bfloat16

D_MODEL = 1024
N_HEADS = 4
HEAD_DIM = D_MODEL // N_HEADS
MEM_LEN = 256
A_TAPS = 3
B_TAPS = 31
N_GROUPS = 4
GROUP_SIZE = 8
N_EXPERTS = N_GROUPS * GROUP_SIZE
D_FF = D_MODEL // 2
N_SEG = 9
EPS = 1e-6

LANES = 128
SUBLANES = 8
N_COL = D_MODEL // LANES

PROJ_TM = 2048
PROJ_TN = 1536
MIX_T = 512
A_HIST = 8
B_HIST = 32
A_M = (MIX_T + A_HIST) // SUBLANES
B_M = (MIX_T + B_HIST) // SUBLANES
ROUTE_T = 512
DISPATCH_T = 512
EXPERT_TM = 512
COMBINE_T = 512
ROW_UNROLL = 8
ROUTER_COL0 = N_GROUPS

VMEM_LIMIT = 56 * 1024 * 1024


def _rms(x, g):
    return x * lax.rsqrt(jnp.mean(x * x, axis=-1, keepdims=True) + EPS) * g


def _sigmoid(x):
    return 1.0 / (1.0 + jnp.exp(-x))


def _cols(x):
    return [x[:, c * LANES:(c + 1) * LANES] for c in range(N_COL)]


def _token_tile(ref, r, rows=N_COL):
    return ref.at[pl.ds(pl.multiple_of(r * rows, rows), rows), :]


Z_ROWS = N_COL


def _load_tiled(ref, rows):
    return jnp.concatenate([ref[pl.ds(c, rows, stride=N_COL), :] for c in range(N_COL)], axis=-1)


def _store_tiled(ref, x):
    for c, part in enumerate(_cols(x)):
        ref[pl.ds(c, x.shape[0], stride=N_COL), :] = part


def _kv_kernel(mem_ref, g_ref, w_ref, k_ref, v_ref):
    mn = _rms(mem_ref[0], g_ref[...]).astype(BF16)
    kv = jnp.dot(mn, w_ref[...], preferred_element_type=F32)
    k_ref[0] = (kv[:, :D_MODEL] * (HEAD_DIM ** -0.5)).astype(BF16)
    v_ref[0] = kv[:, D_MODEL:].astype(BF16)


def _kv_proj(mem, g, w_kv):
    b = mem.shape[0]
    return pl.pallas_call(
        _kv_kernel,
        grid=(b,),
        in_specs=[
            pl.BlockSpec((1, MEM_LEN, D_MODEL), lambda i: (i, 0, 0)),
            pl.BlockSpec((1, D_MODEL), lambda i: (0, 0)),
            pl.BlockSpec((D_MODEL, 2 * D_MODEL), lambda i: (0, 0)),
        ],
        out_specs=[
            pl.BlockSpec((1, MEM_LEN, D_MODEL), lambda i: (i, 0, 0)),
            pl.BlockSpec((1, MEM_LEN, D_MODEL), lambda i: (i, 0, 0)),
        ],
        out_shape=[jax.ShapeDtypeStruct((b, MEM_LEN, D_MODEL), BF16)] * 2,
        compiler_params=pltpu.CompilerParams(vmem_limit_bytes=VMEM_LIMIT),
        name="kv_proj",
    )(mem, g, w_kv)


def _proj_in_kernel(x_ref, g_ref, w_ref, o_ref, xn_ref):
    @pl.when(pl.program_id(1) == 0)
    def _():
        xn_ref[...] = _rms(x_ref[...], g_ref[...]).astype(BF16)

    o_ref[...] = jnp.dot(xn_ref[...], w_ref[0].astype(BF16),
                         preferred_element_type=F32).astype(BF16)


def _proj_in(x, g, w_all, layer):
    n = x.shape[0]
    n_in = w_all.shape[2]
    return pl.pallas_call(
        _proj_in_kernel,
        grid=(n // PROJ_TM, n_in // PROJ_TN),
        in_specs=[
            pl.BlockSpec((PROJ_TM, D_MODEL), lambda i, j: (i, 0)),
            pl.BlockSpec((1, D_MODEL), lambda i, j: (0, 0)),
            pl.BlockSpec((1, D_MODEL, PROJ_TN), lambda i, j: (layer, 0, j)),
        ],
        out_specs=pl.BlockSpec((PROJ_TM, PROJ_TN), lambda i, j: (i, j)),
        out_shape=jax.ShapeDtypeStruct((n, n_in), BF16),
        scratch_shapes=[pltpu.VMEM((PROJ_TM, D_MODEL), BF16)],
        compiler_params=pltpu.CompilerParams(
            dimension_semantics=("arbitrary", "arbitrary"), vmem_limit_bytes=VMEM_LIMIT),
        name="proj_in",
    )(x, g, w_all)


def _causal_conv(src, wrap, dst, taps, bias, col, m, n_taps):
    first_wrap = m - (n_taps - 1)
    view = lambda ref, j: ref[col, pl.ds(j, SUBLANES, stride=m), :]
    for j in range(first_wrap, m):
        wrap[col, pl.ds((j - first_wrap) * SUBLANES, SUBLANES), :] = pltpu.roll(view(src, j), 1, 0)
    w = [jnp.broadcast_to(taps[col, k:k + 1, :], (SUBLANES, LANES)) for k in range(n_taps)]
    for i in range(m):
        acc = None if bias is None else jnp.broadcast_to(bias[col], (SUBLANES, LANES))
        for s in range(n_taps):
            j = i - s
            if j >= 0:
                x = view(src, j)
            else:
                x = wrap[col, pl.ds((j + m - first_wrap) * SUBLANES, SUBLANES), :]
            term = w[n_taps - 1 - s] * x
            acc = term if acc is None else acc + term
        dst[col, pl.ds(i, SUBLANES, stride=m), :] = acc


def _mixer_kernel(aval, agb, agc, gluv, glug, q_ref, g0, g1, g2, x_ref, k_ref, v_ref,
                  caw, cbw, bglu, cbb, lng, lnb, bob, bgate, woa, wob, wox, wo,
                  o_ref, a_src, a_wrap, a_dst, b_src, b_wrap, b_dst):
    t = MIX_T

    @pl.when(pl.program_id(1) == 0)
    def _():
        a_src[:, 0:A_HIST, :] = jnp.zeros((N_COL, A_HIST, LANES), F32)
        b_src[:, 0:B_HIST, :] = jnp.zeros((N_COL, B_HIST, LANES), F32)

    cv = agc[...].astype(F32) * aval[...].astype(F32)
    for c, part in enumerate(_cols(cv)):
        a_src[c, A_HIST:A_HIST + t, :] = part
    for c in range(N_COL):
        _causal_conv(a_src, a_wrap, a_dst, caw, None, c, A_M, A_TAPS)
    conv_a = jnp.concatenate([a_dst[c, A_HIST:A_HIST + t, :] for c in range(N_COL)], axis=-1)
    a_src[:, 0:A_HIST, :] = a_src[:, t:t + A_HIST, :]
    ya_in = (agb[...].astype(F32) * conv_a).astype(BF16)
    y_a = jnp.dot(ya_in, woa[...], preferred_element_type=F32)
    merged = _sigmoid(g0[...].astype(F32) + bgate[:, 0:D_MODEL]) * y_a

    gv = gluv[...].astype(F32) + bglu[:, 0:D_MODEL]
    gg = glug[...].astype(F32) + bglu[:, D_MODEL:2 * D_MODEL]
    for c, part in enumerate(_cols(gv * _sigmoid(gg))):
        b_src[c, B_HIST:B_HIST + t, :] = part

    def conv_col(c, carry):
        _causal_conv(b_src, b_wrap, b_dst, cbw, cbb, c, B_M, B_TAPS)
        return carry

    lax.fori_loop(0, N_COL, conv_col, 0)
    conv_b = jnp.concatenate([b_dst[c, B_HIST:B_HIST + t, :] for c in range(N_COL)], axis=-1)
    b_src[:, 0:B_HIST, :] = b_src[:, t:t + B_HIST, :]
    mu = jnp.mean(conv_b, axis=-1, keepdims=True)
    xc = conv_b - mu
    var = jnp.mean(xc * xc, axis=-1, keepdims=True)
    ln = xc * lax.rsqrt(var + EPS) * lng[...] + lnb[...]
    y_b = jnp.dot((ln * _sigmoid(ln)).astype(BF16), wob[...], preferred_element_type=F32) + bob[...]
    merged += _sigmoid(g1[...].astype(F32) + bgate[:, D_MODEL:2 * D_MODEL]) * y_b

    heads = []
    for h in range(N_HEADS):
        sl = slice(h * HEAD_DIM, (h + 1) * HEAD_DIM)
        s = lax.dot_general(q_ref[:, sl], k_ref[0, :, sl], (((1,), (1,)), ((), ())),
                            preferred_element_type=F32)
        p = jnp.exp(s - jnp.max(s, axis=-1, keepdims=True))
        denom = jnp.sum(p, axis=-1, keepdims=True)
        o = jnp.dot(p.astype(BF16), v_ref[0, :, sl], preferred_element_type=F32)
        heads.append((o / denom).astype(BF16))
    attn = jnp.concatenate(heads, axis=-1)
    y_x = jnp.dot(attn, wox[...], preferred_element_type=F32)
    merged += _sigmoid(g2[...].astype(F32) + bgate[:, 2 * D_MODEL:3 * D_MODEL]) * y_x

    o_ref[...] = x_ref[...] + jnp.dot(merged.astype(BF16), wo[...], preferred_element_type=F32)


def _mixer(h, x, k, v, caw, cbw, bglu, cbb, lng, lnb, bob, bgate, woa, wob, wox, wo, batch):
    n = x.shape[0]
    n_s = n // batch // MIX_T
    row = lambda b, s: b * n_s + s
    seg_specs = [pl.BlockSpec((MIX_T, D_MODEL), functools.partial(lambda b, s, c: (row(b, s), c), c=c))
                 for c in range(N_SEG)]
    const2 = lambda shape: pl.BlockSpec(shape, lambda b, s: (0, 0))
    const3 = lambda shape: pl.BlockSpec(shape, lambda b, s: (0, 0, 0))
    kv_spec = pl.BlockSpec((1, MEM_LEN, D_MODEL), lambda b, s: (b, 0, 0))
    w_spec = pl.BlockSpec((D_MODEL, D_MODEL), lambda b, s: (0, 0), pipeline_mode=pl.Buffered(1))
    col_buf = lambda rows: pltpu.VMEM((N_COL, rows, LANES), F32)
    return pl.pallas_call(
        _mixer_kernel,
        grid=(batch, n_s),
        in_specs=seg_specs + [
            pl.BlockSpec((MIX_T, D_MODEL), lambda b, s: (row(b, s), 0)),
            kv_spec, kv_spec,
            const3((N_COL, A_TAPS, LANES)), const3((N_COL, B_TAPS, LANES)),
            const2((1, 2 * D_MODEL)), const3((N_COL, 1, LANES)), const2((1, D_MODEL)),
            const2((1, D_MODEL)), const2((1, D_MODEL)), const2((1, 3 * D_MODEL)),
            w_spec, w_spec, w_spec, w_spec,
        ],
        out_specs=pl.BlockSpec((MIX_T, D_MODEL), lambda b, s: (row(b, s), 0)),
        out_shape=jax.ShapeDtypeStruct((n, D_MODEL), F32),
        scratch_shapes=[
            col_buf(SUBLANES * A_M), col_buf(SUBLANES * (A_TAPS - 1)), col_buf(SUBLANES * A_M),
            col_buf(SUBLANES * B_M), col_buf(SUBLANES * (B_TAPS - 1)), col_buf(SUBLANES * B_M),
        ],
        compiler_params=pltpu.CompilerParams(
            dimension_semantics=("arbitrary", "arbitrary"), vmem_limit_bytes=VMEM_LIMIT),
        name="mixer",
    )(*([h] * N_SEG), x, k, v, caw, cbw, bglu, cbb, lng, lnb, bob, bgate, woa, wob, wox, wo)


def _route_kernel(x_ref, g_ref, w_ref, b_ref, z_ref, meta_ref, cnt_ref, cnt_acc):
    t = ROUTE_T

    @pl.when(pl.program_id(0) == 0)
    def _():
        cnt_acc[...] = jnp.zeros_like(cnt_acc)

    z = _rms(x_ref[...], g_ref[...])
    _store_tiled(z_ref, z)
    w = w_ref[...]
    z_hi = z.astype(BF16)
    z_lo = (z - z_hi.astype(F32)).astype(BF16)
    w_hi = w.astype(BF16)
    w_lo = (w - w_hi.astype(F32)).astype(BF16)
    logits = (jnp.dot(z_hi, w_hi, preferred_element_type=F32)
              + jnp.dot(z_lo, w_hi, preferred_element_type=F32)
              + jnp.dot(z_hi, w_lo, preferred_element_type=F32)) + b_ref[...]
    lane = lax.broadcasted_iota(jnp.int32, (t, LANES), 1).astype(F32)
    neg = jnp.float32(-jnp.inf)
    big = jnp.float32(1e9)

    gmask = lane < N_GROUPS
    gl = jnp.where(gmask, logits, neg)
    gmax = jnp.max(gl, axis=-1, keepdims=True)
    g_sel = jnp.min(jnp.where(gl == gmax, lane, big), axis=-1, keepdims=True)
    g_p = 1.0 / jnp.sum(jnp.where(gmask, jnp.exp(logits - gmax), 0.0), axis=-1, keepdims=True)

    lo = ROUTER_COL0 + GROUP_SIZE * g_sel
    el = jnp.where((lane >= lo) & (lane < lo + GROUP_SIZE), logits, neg)
    m1 = jnp.max(el, axis=-1, keepdims=True)
    i1 = jnp.min(jnp.where(el == m1, lane, big), axis=-1, keepdims=True)
    el2 = jnp.where(lane == i1, neg, el)
    m2 = jnp.max(el2, axis=-1, keepdims=True)
    i2 = jnp.min(jnp.where(el2 == m2, lane, big), axis=-1, keepdims=True)
    r = jnp.exp(m2 - m1)
    c1 = g_p / (1.0 + r)
    c2 = g_p * r / (1.0 + r)

    oh1 = lane == i1
    oh2 = lane == i2
    ri = lax.broadcasted_iota(jnp.int32, (t, t), 0)
    ci = lax.broadcasted_iota(jnp.int32, (t, t), 1)
    tri = jnp.where(ci < ri, 1.0, 0.0).astype(BF16)
    oh1f = jnp.where(oh1, 1.0, 0.0)
    oh2f = jnp.where(oh2, 1.0, 0.0)
    pre1 = jnp.dot(tri, oh1f.astype(BF16), preferred_element_type=F32)
    pre2 = jnp.dot(tri, oh2f.astype(BF16), preferred_element_type=F32)
    tot1 = jnp.sum(oh1f, axis=0, keepdims=True)
    tot2 = jnp.sum(oh2f, axis=0, keepdims=True)
    cnt = cnt_acc[...]
    rank1 = jnp.sum(oh1f * (pre1 + cnt), axis=-1, keepdims=True)
    rank2 = jnp.sum(oh2f * (pre2 + tot1 + cnt), axis=-1, keepdims=True)
    cnt_new = cnt + tot1 + tot2
    cnt_acc[...] = cnt_new
    cnt_ref[...] = cnt_new

    meta = jnp.where(lane == 0, i1 - ROUTER_COL0, 0.0)
    meta = jnp.where(lane == 1, i2 - ROUTER_COL0, meta)
    meta = jnp.where(lane == 2, rank1, meta)
    meta = jnp.where(lane == 3, rank2, meta)
    meta = jnp.where(lane == 4, c1, meta)
    meta = jnp.where(lane == 5, c2, meta)
    meta_ref[...] = meta


def _route(x, g, w, b):
    n = x.shape[0]
    return pl.pallas_call(
        _route_kernel,
        grid=(n // ROUTE_T,),
        in_specs=[
            pl.BlockSpec((ROUTE_T, D_MODEL), lambda i: (i, 0)),
            pl.BlockSpec((1, D_MODEL), lambda i: (0, 0)),
            pl.BlockSpec((D_MODEL, LANES), lambda i: (0, 0)),
            pl.BlockSpec((1, LANES), lambda i: (0, 0)),
        ],
        out_specs=[
            pl.BlockSpec((ROUTE_T * Z_ROWS, LANES), lambda i: (i, 0)),
            pl.BlockSpec((ROUTE_T, LANES), lambda i: (i, 0)),
            pl.BlockSpec((1, LANES), lambda i: (0, 0)),
        ],
        out_shape=[
            jax.ShapeDtypeStruct((n * Z_ROWS, LANES), F32),
            jax.ShapeDtypeStruct((n, LANES), F32),
            jax.ShapeDtypeStruct((1, LANES), F32),
        ],
        scratch_shapes=[pltpu.VMEM((1, LANES), F32)],
        compiler_params=pltpu.CompilerParams(
            dimension_semantics=("arbitrary",), vmem_limit_bytes=VMEM_LIMIT),
        name="route",
    )(x, g, w, b)


def _zero_pad_slots(pad_start_ref, pad_len_ref, n_used_ref, zeros_ref, out_ref, sem):
    half = EXPERT_TM // 2
    sizes = [1 << b for b in reversed(range(EXPERT_TM.bit_length() - 1))]
    n_tiles = out_ref.shape[0] // (EXPERT_TM * Z_ROWS)

    def zero_copy(at, size):
        return pltpu.make_async_copy(
            zeros_ref.at[pl.ds(0, size * Z_ROWS), :],
            out_ref.at[pl.ds(pl.multiple_of(at * Z_ROWS, Z_ROWS), size * Z_ROWS), :], sem)

    def pad_copies(e, act):
        at = pad_start_ref[e]
        for size in sizes:
            take = (pad_len_ref[e] & size) != 0

            @pl.when(take)
            def _(at=at, size=size):
                act(zero_copy(at, size))

            at = at + jnp.where(take, size, 0)

    def tail_copies(j, act):
        act(zero_copy(j * EXPERT_TM, half))
        act(zero_copy(j * EXPERT_TM + half, half))

    def loops(act):
        def pad_body(e, carry):
            pad_copies(e, act)
            return carry

        def tail_body(j, carry):
            tail_copies(j, act)
            return carry

        lax.fori_loop(0, N_EXPERTS, pad_body, 0)
        lax.fori_loop(n_used_ref[0], n_tiles, tail_body, 0)

    loops(lambda cp: cp.start())
    loops(lambda cp: cp.wait())


def _dispatch_kernel(pos_ref, pad_start_ref, pad_len_ref, n_used_ref, z_ref, out_ref, zeros_ref,
                     sem, pad_sem):
    @pl.when(pl.program_id(0) == 0)
    def _():
        zeros_ref[...] = jnp.zeros_like(zeros_ref)
        _zero_pad_slots(pad_start_ref, pad_len_ref, n_used_ref, zeros_ref, out_ref, pad_sem)

    def start(g, carry):
        for u in range(ROW_UNROLL):
            r = g * ROW_UNROLL + u
            for k in range(2):
                pltpu.make_async_copy(
                    _token_tile(z_ref, r, Z_ROWS),
                    _token_tile(out_ref, pos_ref[2 * r + k], Z_ROWS), sem).start(priority=k)
        return carry

    lax.fori_loop(0, DISPATCH_T // ROW_UNROLL, start, 0)
    for _ in range(2):
        pltpu.make_async_copy(z_ref, out_ref.at[pl.ds(0, DISPATCH_T * Z_ROWS), :], sem).wait()


def _dispatch(pos_flat, pad_start, pad_len, n_used, z, n_slots):
    n = z.shape[0] // Z_ROWS
    return pl.pallas_call(
        _dispatch_kernel,
        grid=(n // DISPATCH_T,),
        in_specs=[
            pl.BlockSpec((2 * DISPATCH_T,), lambda i: (i,), memory_space=pltpu.SMEM),
            pl.BlockSpec(memory_space=pltpu.SMEM),
            pl.BlockSpec(memory_space=pltpu.SMEM),
            pl.BlockSpec(memory_space=pltpu.SMEM),
            pl.BlockSpec((DISPATCH_T * Z_ROWS, LANES), lambda i: (i, 0)),
        ],
        out_specs=pl.BlockSpec(memory_space=pl.ANY),
        out_shape=jax.ShapeDtypeStruct((n_slots * Z_ROWS, LANES), F32),
        scratch_shapes=[
            pltpu.VMEM((EXPERT_TM // 2 * Z_ROWS, LANES), F32),
            pltpu.SemaphoreType.DMA,
            pltpu.SemaphoreType.DMA,
        ],
        compiler_params=pltpu.CompilerParams(
            dimension_semantics=("arbitrary",), vmem_limit_bytes=VMEM_LIMIT),
        name="dispatch",
    )(pos_flat, pad_start, pad_len, n_used, z)


def _experts_kernel(te_ref, nu_ref, z_ref, wgu_ref, wd_ref, y_ref, wgu_bf, wd_bf):
    j = pl.program_id(0)
    used = j < nu_ref[0]
    new_expert = (j == 0) | (te_ref[j] != te_ref[jnp.maximum(j - 1, 0)])

    @pl.when(used & new_expert)
    def _():
        wgu_bf[...] = wgu_ref[0, 0].astype(BF16)
        wd_bf[...] = wd_ref[0, 0].astype(BF16)

    @pl.when(used)
    def _():
        z = _load_tiled(z_ref, EXPERT_TM).astype(BF16)
        gu = jnp.dot(z, wgu_bf[...], preferred_element_type=F32)
        gate = gu[:, :D_FF]
        up = gu[:, D_FF:]
        hmid = (gate * _sigmoid(gate) * up).astype(BF16)
        _store_tiled(y_ref, jnp.dot(hmid, wd_bf[...], preferred_element_type=F32))

    @pl.when(jnp.logical_not(used))
    def _():
        y_ref[...] = jnp.zeros_like(y_ref)


def _experts(tile_expert, n_used, zs, wgu_all, wd_all, layer):
    n_slots = zs.shape[0] // Z_ROWS
    n_tiles = n_slots // EXPERT_TM
    last = lambda j, nu: jnp.maximum(jnp.minimum(j, nu[0] - 1), 0)
    return pl.pallas_call(
        _experts_kernel,
        grid_spec=pltpu.PrefetchScalarGridSpec(
            num_scalar_prefetch=2,
            grid=(n_tiles,),
            in_specs=[
                pl.BlockSpec((EXPERT_TM * Z_ROWS, LANES), lambda j, te, nu: (last(j, nu), 0)),
                pl.BlockSpec((1, 1, D_MODEL, 2 * D_FF), lambda j, te, nu: (layer, te[j], 0, 0)),
                pl.BlockSpec((1, 1, D_FF, D_MODEL), lambda j, te, nu: (layer, te[j], 0, 0)),
            ],
            out_specs=pl.BlockSpec((EXPERT_TM * N_COL, LANES), lambda j, te, nu: (j, 0)),
            scratch_shapes=[pltpu.VMEM((D_MODEL, 2 * D_FF), BF16), pltpu.VMEM((D_FF, D_MODEL), BF16)],
        ),
        out_shape=jax.ShapeDtypeStruct((n_slots * N_COL, LANES), F32),
        compiler_params=pltpu.CompilerParams(
            dimension_semantics=("arbitrary",), vmem_limit_bytes=VMEM_LIMIT),
        name="experts",
    )(tile_expert, n_used, zs, wgu_all, wd_all)


def _combine_kernel(pos_ref, pos_next_ref, x_ref, meta_ref, g_ref, y_ref, o_ref, buf0, buf1, sems,
                    *, final_norm):
    bufs = (buf0, buf1)
    i = pl.program_id(0)
    slot = i % 2

    def gather(pos, into):
        def start(g, carry):
            for u in range(ROW_UNROLL):
                r = g * ROW_UNROLL + u
                for k in range(2):
                    pltpu.make_async_copy(
                        _token_tile(y_ref, pos[2 * r + k]), _token_tile(bufs[k].at[into], r),
                        sems.at[into]).start(priority=k)
            return carry

        lax.fori_loop(0, COMBINE_T // ROW_UNROLL, start, 0)

    @pl.when(i == 0)
    def _():
        gather(pos_ref, 0)

    @pl.when(i + 1 < pl.num_programs(0))
    def _():
        gather(pos_next_ref, 1 - slot)

    for k in range(2):
        pltpu.make_async_copy(y_ref.at[pl.ds(0, COMBINE_T * N_COL), :], bufs[k].at[slot],
                              sems.at[slot]).wait()
    meta = meta_ref[...]
    y0 = _load_tiled(buf0.at[slot], COMBINE_T)
    y1 = _load_tiled(buf1.at[slot], COMBINE_T)
    out = x_ref[...] + meta[:, 4:5] * y0 + meta[:, 5:6] * y1
    if final_norm:
        out = _rms(out, g_ref[...])
    o_ref[...] = out


def _combine(pos_flat, x, meta, g, y, final_norm):
    n = x.shape[0]
    n_steps = n // COMBINE_T
    return pl.pallas_call(
        functools.partial(_combine_kernel, final_norm=final_norm),
        grid=(n_steps,),
        in_specs=[
            pl.BlockSpec((2 * COMBINE_T,), lambda i: (i,), memory_space=pltpu.SMEM),
            pl.BlockSpec((2 * COMBINE_T,), lambda i: (jnp.minimum(i + 1, n_steps - 1),),
                         memory_space=pltpu.SMEM),
            pl.BlockSpec((COMBINE_T, D_MODEL), lambda i: (i, 0)),
            pl.BlockSpec((COMBINE_T, LANES), lambda i: (i, 0)),
            pl.BlockSpec((1, D_MODEL), lambda i: (0, 0)),
            pl.BlockSpec(memory_space=pl.ANY),
        ],
        out_specs=pl.BlockSpec((COMBINE_T, D_MODEL), lambda i: (i, 0)),
        out_shape=jax.ShapeDtypeStruct((n, D_MODEL), F32),
        scratch_shapes=[
            pltpu.VMEM((2, COMBINE_T * N_COL, LANES), F32),
            pltpu.VMEM((2, COMBINE_T * N_COL, LANES), F32),
            pltpu.SemaphoreType.DMA((2,)),
        ],
        compiler_params=pltpu.CompilerParams(
            dimension_semantics=("arbitrary",), vmem_limit_bytes=VMEM_LIMIT),
        name="combine",
    )(pos_flat, pos_flat, x, meta, g, y)


def _slot_plan(meta, counts, n_tiles):
    cnt = counts[0, ROUTER_COL0:ROUTER_COL0 + N_EXPERTS].astype(jnp.int32)
    padded = (cnt + EXPERT_TM - 1) // EXPERT_TM * EXPERT_TM
    ends = jnp.cumsum(padded)
    offs = ends - padded
    e = meta[:, 0:2].astype(jnp.int32)
    rank = meta[:, 2:4].astype(jnp.int32)
    onehot = e[..., None] == jnp.arange(N_EXPERTS, dtype=jnp.int32)
    pos = jnp.sum(jnp.where(onehot, offs, 0), axis=-1) + rank
    tile_ends = ends // EXPERT_TM
    tile_ids = jnp.arange(n_tiles, dtype=jnp.int32)
    tile_expert = jnp.sum(tile_ids[:, None] >= tile_ends[None, :], axis=1).astype(jnp.int32)
    tile_expert = jnp.minimum(tile_expert, N_EXPERTS - 1)
    n_used = tile_ends[-1:].astype(jnp.int32)
    return pos.reshape(-1), tile_expert, n_used, offs + cnt, padded - cnt


def _col_major(w):
    return w.reshape(w.shape[0], N_COL, LANES).transpose(1, 0, 2)


def kernel(x, mem, mix_norm_g, mem_norm_g, w_in, b_gate, conv_a_w, w_out_a, b_glu, conv_b_w,
           conv_b_b, ln_b_g, ln_b_b, w_out_b, b_out_b, w_kv, w_out_x, w_o, ffn_norm_g, w_group,
           b_group, w_router, b_router, w_gate_up, w_down, final_norm_g):
    batch, seq, d = x.shape
    depth = w_in.shape[0]
    n = batch * seq
    n_tiles = (2 * n) // EXPERT_TM + N_EXPERTS
    n_slots = n_tiles * EXPERT_TM
    row = lambda a: a.reshape(1, -1)
    xf = x.reshape(n, d)
    pad = LANES - N_GROUPS - N_EXPERTS
    for l in range(depth):
        k, v = _kv_proj(mem, row(mem_norm_g[l]), w_kv[l].astype(BF16))
        h = _proj_in(xf, row(mix_norm_g[l]), w_in, l)
        x1 = _mixer(h, xf, k, v, _col_major(conv_a_w[l]), _col_major(conv_b_w[l]), row(b_glu[l]),
                    _col_major(row(conv_b_b[l])), row(ln_b_g[l]), row(ln_b_b[l]), row(b_out_b[l]),
                    row(b_gate[l]), w_out_a[l].astype(BF16), w_out_b[l].astype(BF16),
                    w_out_x[l].astype(BF16), w_o[l].astype(BF16), batch)
        w_r = jnp.pad(jnp.concatenate([w_group[l], w_router[l]], axis=1), ((0, 0), (0, pad)))
        b_r = jnp.pad(jnp.concatenate([b_group[l], b_router[l]]), (0, pad)).reshape(1, LANES)
        z, meta, counts = _route(x1, row(ffn_norm_g[l]), w_r, b_r)
        pos, tile_expert, n_used, pad_start, pad_len = _slot_plan(meta, counts, n_tiles)
        zs = _dispatch(pos, pad_start, pad_len, n_used, z, n_slots)
        y = _experts(tile_expert, n_used, zs, w_gate_up, w_down, l)
        xf = _combine(pos, x1, meta, row(final_norm_g), y, final_norm=(l == depth - 1))
    return xf.reshape(batch, seq, d)
```

```python
import functools

import jax
import jax.numpy as jnp
from jax import lax
from jax.experimental import pallas as pl
from jax.experimental.pallas import tpu as pltpu

F32 = jnp.float32
BF16 = jnp.bfloat16

D_MODEL = 1024
N_HEADS = 4
HEAD_DIM = D_MODEL // N_HEADS
MEM_LEN = 256
A_TAPS = 3
B_TAPS = 31
N_GROUPS = 4
GROUP_SIZE = 8
N_EXPERTS = N_GROUPS * GROUP_SIZE
D_FF = D_MODEL // 2
N_SEG = 9
EPS = 1e-6

LANES = 128
SUBLANES = 8
N_COL = D_MODEL // LANES

PROJ_TM = 2048
PROJ_TN = 1536
MIX_T = 512
A_HIST = 8
B_HIST = 32
A_M = (MIX_T + A_HIST) // SUBLANES
B_M = (MIX_T + B_HIST) // SUBLANES
ROUTE_T = 512
DISPATCH_T = 512
EXPERT_TM = 512
COMBINE_T = 512
ROW_UNROLL = 8
ROUTER_COL0 = N_GROUPS

VMEM_LIMIT = 56 * 1024 * 1024
MIXER_VMEM_LIMIT = 62 * 1024 * 1024


def _rms(x, g):
    return x * lax.rsqrt(jnp.mean(x * x, axis=-1, keepdims=True) + EPS) * g


def _sigmoid(x):
    return 1.0 / (1.0 + jnp.exp(-x))


def _cols(x):
    return [x[:, c * LANES:(c + 1) * LANES] for c in range(N_COL)]


def _token_tile(ref, r, rows=N_COL):
    return ref.at[pl.ds(pl.multiple_of(r * rows, rows), rows), :]


Z_ROWS = N_COL


def _load_tiled(ref, rows):
    return jnp.concatenate([ref[pl.ds(c, rows, stride=N_COL), :] for c in range(N_COL)], axis=-1)


def _store_tiled(ref, x):
    for c, part in enumerate(_cols(x)):
        ref[pl.ds(c, x.shape[0], stride=N_COL), :] = part


def _kv_kernel(mem_ref, g_ref, w_ref, k_ref, v_ref):
    mn = _rms(mem_ref[0], g_ref[...]).astype(BF16)
    kv = jnp.dot(mn, w_ref[...], preferred_element_type=F32)
    k_ref[0] = (kv[:, :D_MODEL] * (HEAD_DIM ** -0.5)).astype(BF16)
    v_ref[0] = kv[:, D_MODEL:].astype(BF16)


def _kv_proj(mem, g, w_kv):
    b = mem.shape[0]
    return pl.pallas_call(
        _kv_kernel,
        grid=(b,),
        in_specs=[
            pl.BlockSpec((1, MEM_LEN, D_MODEL), lambda i: (i, 0, 0)),
            pl.BlockSpec((1, D_MODEL), lambda i: (0, 0)),
            pl.BlockSpec((D_MODEL, 2 * D_MODEL), lambda i: (0, 0)),
        ],
        out_specs=[
            pl.BlockSpec((1, MEM_LEN, D_MODEL), lambda i: (i, 0, 0)),
            pl.BlockSpec((1, MEM_LEN, D_MODEL), lambda i: (i, 0, 0)),
        ],
        out_shape=[jax.ShapeDtypeStruct((b, MEM_LEN, D_MODEL), BF16)] * 2,
        compiler_params=pltpu.CompilerParams(vmem_limit_bytes=VMEM_LIMIT),
        name="kv_proj",
    )(mem, g, w_kv)


def _proj_in_kernel(x_ref, g_ref, w_ref, o_ref, xn_ref):
    @pl.when(pl.program_id(1) == 0)
    def _():
        xn_ref[...] = _rms(x_ref[...], g_ref[...]).astype(BF16)

    o_ref[...] = jnp.dot(xn_ref[...], w_ref[0].astype(BF16),
                         preferred_element_type=F32).astype(BF16)


def _proj_in(x, g, w_all, layer):
    n = x.shape[0]
    n_in = w_all.shape[2]
    return pl.pallas_call(
        _proj_in_kernel,
        grid=(n // PROJ_TM, n_in // PROJ_TN),
        in_specs=[
            pl.BlockSpec((PROJ_TM, D_MODEL), lambda i, j: (i, 0)),
            pl.BlockSpec((1, D_MODEL), lambda i, j: (0, 0)),
            pl.BlockSpec((1, D_MODEL, PROJ_TN), lambda i, j: (layer, 0, j)),
        ],
        out_specs=pl.BlockSpec((PROJ_TM, PROJ_TN), lambda i, j: (i, j)),
        out_shape=jax.ShapeDtypeStruct((n, n_in), BF16),
        scratch_shapes=[pltpu.VMEM((PROJ_TM, D_MODEL), BF16)],
        compiler_params=pltpu.CompilerParams(
            dimension_semantics=("arbitrary", "arbitrary"), vmem_limit_bytes=VMEM_LIMIT),
        name="proj_in",
    )(x, g, w_all)


def _causal_conv(src, wrap, dst, taps, bias, col, m, n_taps):
    first_wrap = m - (n_taps - 1)
    view = lambda ref, j: ref[col, pl.ds(j, SUBLANES, stride=m), :]
    for j in range(first_wrap, m):
        wrap[col, pl.ds((j - first_wrap) * SUBLANES, SUBLANES), :] = pltpu.roll(view(src, j), 1, 0)
    w = [jnp.broadcast_to(taps[col, k:k + 1, :], (SUBLANES, LANES)) for k in range(n_taps)]
    for i in range(m):
        acc = None if bias is None else jnp.broadcast_to(bias[col], (SUBLANES, LANES))
        for s in range(n_taps):
            j = i - s
            if j >= 0:
                x = view(src, j)
            else:
                x = wrap[col, pl.ds((j + m - first_wrap) * SUBLANES, SUBLANES), :]
            term = w[n_taps - 1 - s] * x
            acc = term if acc is None else acc + term
        dst[col, pl.ds(i, SUBLANES, stride=m), :] = acc


def _mixer_kernel(aval, agb, agc, gluv, glug, q_ref, g0, g1, g2, x_ref, k_ref, v_ref,
                  caw, cbw, bglu, cbb, lng, lnb, bob, bgate, woa, wob, wox, wo, gffn, wr, br,
                  o_ref, z_ref, meta_ref, cnt_ref,
                  a_src, a_wrap, a_dst, b_src, b_wrap, b_dst, cnt_acc):
    t = MIX_T

    @pl.when(pl.program_id(1) == 0)
    def _():
        a_src[:, 0:A_HIST, :] = jnp.zeros((N_COL, A_HIST, LANES), F32)
        b_src[:, 0:B_HIST, :] = jnp.zeros((N_COL, B_HIST, LANES), F32)

    cv = agc[...].astype(F32) * aval[...].astype(F32)
    for c, part in enumerate(_cols(cv)):
        a_src[c, A_HIST:A_HIST + t, :] = part
    for c in range(N_COL):
        _causal_conv(a_src, a_wrap, a_dst, caw, None, c, A_M, A_TAPS)
    conv_a = jnp.concatenate([a_dst[c, A_HIST:A_HIST + t, :] for c in range(N_COL)], axis=-1)
    a_src[:, 0:A_HIST, :] = a_src[:, t:t + A_HIST, :]
    ya_in = (agb[...].astype(F32) * conv_a).astype(BF16)
    y_a = jnp.dot(ya_in, woa[...], preferred_element_type=F32)
    merged = _sigmoid(g0[...].astype(F32) + bgate[:, 0:D_MODEL]) * y_a

    gv = gluv[...].astype(F32) + bglu[:, 0:D_MODEL]
    gg = glug[...].astype(F32) + bglu[:, D_MODEL:2 * D_MODEL]
    for c, part in enumerate(_cols(gv * _sigmoid(gg))):
        b_src[c, B_HIST:B_HIST + t, :] = part

    def conv_col(c, carry):
        _causal_conv(b_src, b_wrap, b_dst, cbw, cbb, c, B_M, B_TAPS)
        return carry

    lax.fori_loop(0, N_COL, conv_col, 0)
    conv_b = jnp.concatenate([b_dst[c, B_HIST:B_HIST + t, :] for c in range(N_COL)], axis=-1)
    b_src[:, 0:B_HIST, :] = b_src[:, t:t + B_HIST, :]
    mu = jnp.mean(conv_b, axis=-1, keepdims=True)
    xc = conv_b - mu
    var = jnp.mean(xc * xc, axis=-1, keepdims=True)
    ln = xc * lax.rsqrt(var + EPS) * lng[...] + lnb[...]
    y_b = jnp.dot((ln * _sigmoid(ln)).astype(BF16), wob[...], preferred_element_type=F32) + bob[...]
    merged += _sigmoid(g1[...].astype(F32) + bgate[:, D_MODEL:2 * D_MODEL]) * y_b

    heads = []
    for h in range(N_HEADS):
        sl = slice(h * HEAD_DIM, (h + 1) * HEAD_DIM)
        s = lax.dot_general(q_ref[:, sl], k_ref[0, :, sl], (((1,), (1,)), ((), ())),
                            preferred_element_type=F32)
        p = jnp.exp(s - jnp.max(s, axis=-1, keepdims=True))
        denom = jnp.sum(p, axis=-1, keepdims=True)
        o = jnp.dot(p.astype(BF16), v_ref[0, :, sl], preferred_element_type=F32)
        heads.append((o / denom).astype(BF16))
    attn = jnp.concatenate(heads, axis=-1)
    y_x = jnp.dot(attn, wox[...], preferred_element_type=F32)
    merged += _sigmoid(g2[...].astype(F32) + bgate[:, 2 * D_MODEL:3 * D_MODEL]) * y_x

    x1 = x_ref[...] + jnp.dot(merged.astype(BF16), wo[...], preferred_element_type=F32)
    o_ref[...] = x1
    first = (pl.program_id(0) == 0) & (pl.program_id(1) == 0)
    _route_rows(x1, first, gffn, wr, br, z_ref, meta_ref, cnt_ref, cnt_acc)


def _mixer(h, x, k, v, caw, cbw, bglu, cbb, lng, lnb, bob, bgate, woa, wob, wox, wo, gffn, wr, br,
           batch):
    assert MIX_T == ROUTE_T
    n = x.shape[0]
    n_s = n // batch // MIX_T
    row = lambda b, s: b * n_s + s
    seg_specs = [pl.BlockSpec((MIX_T, D_MODEL), functools.partial(lambda b, s, c: (row(b, s), c), c=c))
                 for c in range(N_SEG)]
    const2 = lambda shape: pl.BlockSpec(shape, lambda b, s: (0, 0))
    const3 = lambda shape: pl.BlockSpec(shape, lambda b, s: (0, 0, 0))
    kv_spec = pl.BlockSpec((1, MEM_LEN, D_MODEL), lambda b, s: (b, 0, 0))
    w_spec = pl.BlockSpec((D_MODEL, D_MODEL), lambda b, s: (0, 0), pipeline_mode=pl.Buffered(1))
    col_buf = lambda rows: pltpu.VMEM((N_COL, rows, LANES), F32)
    return pl.pallas_call(
        _mixer_kernel,
        grid=(batch, n_s),
        in_specs=seg_specs + [
            pl.BlockSpec((MIX_T, D_MODEL), lambda b, s: (row(b, s), 0)),
            kv_spec, kv_spec,
            const3((N_COL, A_TAPS, LANES)), const3((N_COL, B_TAPS, LANES)),
            const2((1, 2 * D_MODEL)), const3((N_COL, 1, LANES)), const2((1, D_MODEL)),
            const2((1, D_MODEL)), const2((1, D_MODEL)), const2((1, 3 * D_MODEL)),
            w_spec, w_spec, w_spec, w_spec,
            const2((1, D_MODEL)), const2((D_MODEL, LANES)), const2((1, LANES)),
        ],
        out_specs=[
            pl.BlockSpec((MIX_T, D_MODEL), lambda b, s: (row(b, s), 0)),
            pl.BlockSpec((MIX_T * Z_ROWS, LANES), lambda b, s: (row(b, s), 0)),
            pl.BlockSpec((MIX_T, LANES), lambda b, s: (row(b, s), 0)),
            pl.BlockSpec((1, LANES), lambda b, s: (0, 0)),
        ],
        out_shape=[
            jax.ShapeDtypeStruct((n, D_MODEL), F32),
            jax.ShapeDtypeStruct((n * Z_ROWS, LANES), F32),
            jax.ShapeDtypeStruct((n, LANES), F32),
            jax.ShapeDtypeStruct((1, LANES), F32),
        ],
        scratch_shapes=[
            col_buf(SUBLANES * A_M), col_buf(SUBLANES * (A_TAPS - 1)), col_buf(SUBLANES * A_M),
            col_buf(SUBLANES * B_M), col_buf(SUBLANES * (B_TAPS - 1)), col_buf(SUBLANES * B_M),
            pltpu.VMEM((1, LANES), F32),
        ],
        compiler_params=pltpu.CompilerParams(
            dimension_semantics=("arbitrary", "arbitrary"), vmem_limit_bytes=MIXER_VMEM_LIMIT),
        name="mixer",
    )(*([h] * N_SEG), x, k, v, caw, cbw, bglu, cbb, lng, lnb, bob, bgate, woa, wob, wox, wo,
      gffn, wr, br)


def _route_rows(x, first, g_ref, w_ref, b_ref, z_ref, meta_ref, cnt_ref, cnt_acc):
    t = ROUTE_T

    @pl.when(first)
    def _():
        cnt_acc[...] = jnp.zeros_like(cnt_acc)

    z = _rms(x, g_ref[...])
    _store_tiled(z_ref, z)
    w = w_ref[...]
    z_hi = z.astype(BF16)
    z_lo = (z - z_hi.astype(F32)).astype(BF16)
    w_hi = w.astype(BF16)
    w_lo = (w - w_hi.astype(F32)).astype(BF16)
    logits = (jnp.dot(z_hi, w_hi, preferred_element_type=F32)
              + jnp.dot(z_lo, w_hi, preferred_element_type=F32)
              + jnp.dot(z_hi, w_lo, preferred_element_type=F32)) + b_ref[...]
    lane = lax.broadcasted_iota(jnp.int32, (t, LANES), 1).astype(F32)
    neg = jnp.float32(-jnp.inf)
    big = jnp.float32(1e9)

    gmask = lane < N_GROUPS
    gl = jnp.where(gmask, logits, neg)
    gmax = jnp.max(gl, axis=-1, keepdims=True)
    g_sel = jnp.min(jnp.where(gl == gmax, lane, big), axis=-1, keepdims=True)
    g_p = 1.0 / jnp.sum(jnp.where(gmask, jnp.exp(logits - gmax), 0.0), axis=-1, keepdims=True)

    lo = ROUTER_COL0 + GROUP_SIZE * g_sel
    el = jnp.where((lane >= lo) & (lane < lo + GROUP_SIZE), logits, neg)
    m1 = jnp.max(el, axis=-1, keepdims=True)
    i1 = jnp.min(jnp.where(el == m1, lane, big), axis=-1, keepdims=True)
    el2 = jnp.where(lane == i1, neg, el)
    m2 = jnp.max(el2, axis=-1, keepdims=True)
    i2 = jnp.min(jnp.where(el2 == m2, lane, big), axis=-1, keepdims=True)
    r = jnp.exp(m2 - m1)
    c1 = g_p / (1.0 + r)
    c2 = g_p * r / (1.0 + r)

    oh1 = lane == i1
    oh2 = lane == i2
    ri = lax.broadcasted_iota(jnp.int32, (t, t), 0)
    ci = lax.broadcasted_iota(jnp.int32, (t, t), 1)
    tri = jnp.where(ci < ri, 1.0, 0.0).astype(BF16)
    oh1f = jnp.where(oh1, 1.0, 0.0)
    oh2f = jnp.where(oh2, 1.0, 0.0)
    pre1 = jnp.dot(tri, oh1f.astype(BF16), preferred_element_type=F32)
    pre2 = jnp.dot(tri, oh2f.astype(BF16), preferred_element_type=F32)
    tot1 = jnp.sum(oh1f, axis=0, keepdims=True)
    tot2 = jnp.sum(oh2f, axis=0, keepdims=True)
    cnt = cnt_acc[...]
    rank1 = jnp.sum(oh1f * (pre1 + cnt), axis=-1, keepdims=True)
    rank2 = jnp.sum(oh2f * (pre2 + tot1 + cnt), axis=-1, keepdims=True)
    cnt_new = cnt + tot1 + tot2
    cnt_acc[...] = cnt_new
    cnt_ref[...] = cnt_new

    meta = jnp.where(lane == 0, i1 - ROUTER_COL0, 0.0)
    meta = jnp.where(lane == 1, i2 - ROUTER_COL0, meta)
    meta = jnp.where(lane == 2, rank1, meta)
    meta = jnp.where(lane == 3, rank2, meta)
    meta = jnp.where(lane == 4, c1, meta)
    meta = jnp.where(lane == 5, c2, meta)
    meta_ref[...] = meta


def _zero_pad_slots(pad_start_ref, pad_len_ref, n_used_ref, zeros_ref, out_ref, sem):
    half = EXPERT_TM // 2
    sizes = [1 << b for b in reversed(range(EXPERT_TM.bit_length() - 1))]
    n_tiles = out_ref.shape[0] // (EXPERT_TM * Z_ROWS)

    def zero_copy(at, size):
        return pltpu.make_async_copy(
            zeros_ref.at[pl.ds(0, size * Z_ROWS), :],
            out_ref.at[pl.ds(pl.multiple_of(at * Z_ROWS, Z_ROWS), size * Z_ROWS), :], sem)

    def pad_copies(e, act):
        at = pad_start_ref[e]
        for size in sizes:
            take = (pad_len_ref[e] & size) != 0

            @pl.when(take)
            def _(at=at, size=size):
                act(zero_copy(at, size))

            at = at + jnp.where(take, size, 0)

    def tail_copies(j, act):
        act(zero_copy(j * EXPERT_TM, half))
        act(zero_copy(j * EXPERT_TM + half, half))

    def loops(act):
        def pad_body(e, carry):
            pad_copies(e, act)
            return carry

        def tail_body(j, carry):
            tail_copies(j, act)
            return carry

        lax.fori_loop(0, N_EXPERTS, pad_body, 0)
        lax.fori_loop(n_used_ref[0], n_tiles, tail_body, 0)

    loops(lambda cp: cp.start())
    loops(lambda cp: cp.wait())


def _dispatch_kernel(pos_ref, pad_start_ref, pad_len_ref, n_used_ref, z_ref, out_ref, zeros_ref,
                     sem, pad_sem):
    @pl.when(pl.program_id(0) == 0)
    def _():
        zeros_ref[...] = jnp.zeros_like(zeros_ref)
        _zero_pad_slots(pad_start_ref, pad_len_ref, n_used_ref, zeros_ref, out_ref, pad_sem)

    def start(g, carry):
        for u in range(ROW_UNROLL):
            r = g * ROW_UNROLL + u
            for k in range(2):
                pltpu.make_async_copy(
                    _token_tile(z_ref, r, Z_ROWS),
                    _token_tile(out_ref, pos_ref[2 * r + k], Z_ROWS), sem).start(priority=k)
        return carry

    lax.fori_loop(0, DISPATCH_T // ROW_UNROLL, start, 0)
    for _ in range(2):
        pltpu.make_async_copy(z_ref, out_ref.at[pl.ds(0, DISPATCH_T * Z_ROWS), :], sem).wait()


def _dispatch(pos_flat, pad_start, pad_len, n_used, z, n_slots):
    n = z.shape[0] // Z_ROWS
    return pl.pallas_call(
        _dispatch_kernel,
        grid=(n // DISPATCH_T,),
        in_specs=[
            pl.BlockSpec((2 * DISPATCH_T,), lambda i: (i,), memory_space=pltpu.SMEM),
            pl.BlockSpec(memory_space=pltpu.SMEM),
            pl.BlockSpec(memory_space=pltpu.SMEM),
            pl.BlockSpec(memory_space=pltpu.SMEM),
            pl.BlockSpec((DISPATCH_T * Z_ROWS, LANES), lambda i: (i, 0)),
        ],
        out_specs=pl.BlockSpec(memory_space=pl.ANY),
        out_shape=jax.ShapeDtypeStruct((n_slots * Z_ROWS, LANES), F32),
        scratch_shapes=[
            pltpu.VMEM((EXPERT_TM // 2 * Z_ROWS, LANES), F32),
            pltpu.SemaphoreType.DMA,
            pltpu.SemaphoreType.DMA,
        ],
        compiler_params=pltpu.CompilerParams(
            dimension_semantics=("arbitrary",), vmem_limit_bytes=VMEM_LIMIT),
        name="dispatch",
    )(pos_flat, pad_start, pad_len, n_used, z)


def _experts_kernel(te_ref, nu_ref, z_ref, wgu_ref, wd_ref, y_ref, wgu_bf, wd_bf):
    j = pl.program_id(0)
    used = j < nu_ref[0]
    new_expert = (j == 0) | (te_ref[j] != te_ref[jnp.maximum(j - 1, 0)])

    @pl.when(used & new_expert)
    def _():
        wgu_bf[...] = wgu_ref[0, 0].astype(BF16)
        wd_bf[...] = wd_ref[0, 0].astype(BF16)

    @pl.when(used)
    def _():
        z = _load_tiled(z_ref, EXPERT_TM).astype(BF16)
        gu = jnp.dot(z, wgu_bf[...], preferred_element_type=F32)
        gate = gu[:, :D_FF]
        up = gu[:, D_FF:]
        hmid = (gate * _sigmoid(gate) * up).astype(BF16)
        _store_tiled(y_ref, jnp.dot(hmid, wd_bf[...], preferred_element_type=F32))

    @pl.when(jnp.logical_not(used))
    def _():
        y_ref[...] = jnp.zeros_like(y_ref)


def _experts(tile_expert, n_used, zs, wgu_all, wd_all, layer):
    n_slots = zs.shape[0] // Z_ROWS
    n_tiles = n_slots // EXPERT_TM
    last = lambda j, nu: jnp.maximum(jnp.minimum(j, nu[0] - 1), 0)
    return pl.pallas_call(
        _experts_kernel,
        grid_spec=pltpu.PrefetchScalarGridSpec(
            num_scalar_prefetch=2,
            grid=(n_tiles,),
            in_specs=[
                pl.BlockSpec((EXPERT_TM * Z_ROWS, LANES), lambda j, te, nu: (last(j, nu), 0)),
                pl.BlockSpec((1, 1, D_MODEL, 2 * D_FF), lambda j, te, nu: (layer, te[j], 0, 0)),
                pl.BlockSpec((1, 1, D_FF, D_MODEL), lambda j, te, nu: (layer, te[j], 0, 0)),
            ],
            out_specs=pl.BlockSpec((EXPERT_TM * N_COL, LANES), lambda j, te, nu: (j, 0)),
            scratch_shapes=[pltpu.VMEM((D_MODEL, 2 * D_FF), BF16), pltpu.VMEM((D_FF, D_MODEL), BF16)],
        ),
        out_shape=jax.ShapeDtypeStruct((n_slots * N_COL, LANES), F32),
        compiler_params=pltpu.CompilerParams(
            dimension_semantics=("arbitrary",), vmem_limit_bytes=VMEM_LIMIT),
        name="experts",
    )(tile_expert, n_used, zs, wgu_all, wd_all)


def _combine_kernel(pos_ref, pos_next_ref, x_ref, meta_ref, g_ref, y_ref, o_ref, buf0, buf1, sems,
                    *, final_norm):
    bufs = (buf0, buf1)
    i = pl.program_id(0)
    slot = i % 2

    def gather(pos, into):
        def start(g, carry):
            for u in range(ROW_UNROLL):
                r = g * ROW_UNROLL + u
                for k in range(2):
                    pltpu.make_async_copy(
                        _token_tile(y_ref, pos[2 * r + k]), _token_tile(bufs[k].at[into], r),
                        sems.at[into]).start(priority=k)
            return carry

        lax.fori_loop(0, COMBINE_T // ROW_UNROLL, start, 0)

    @pl.when(i == 0)
    def _():
        gather(pos_ref, 0)

    @pl.when(i + 1 < pl.num_programs(0))
    def _():
        gather(pos_next_ref, 1 - slot)

    for k in range(2):
        pltpu.make_async_copy(y_ref.at[pl.ds(0, COMBINE_T * N_COL), :], bufs[k].at[slot],
                              sems.at[slot]).wait()
    meta = meta_ref[...]
    y0 = _load_tiled(buf0.at[slot], COMBINE_T)
    y1 = _load_tiled(buf1.at[slot], COMBINE_T)
    out = x_ref[...] + meta[:, 4:5] * y0 + meta[:, 5:6] * y1
    if final_norm:
        out = _rms(out, g_ref[...])
    o_ref[...] = out


def _combine(pos_flat, x, meta, g, y, final_norm):
    n = x.shape[0]
    n_steps = n // COMBINE_T
    return pl.pallas_call(
        functools.partial(_combine_kernel, final_norm=final_norm),
        grid=(n_steps,),
        in_specs=[
            pl.BlockSpec((2 * COMBINE_T,), lambda i: (i,), memory_space=pltpu.SMEM),
            pl.BlockSpec((2 * COMBINE_T,), lambda i: (jnp.minimum(i + 1, n_steps - 1),),
                         memory_space=pltpu.SMEM),
            pl.BlockSpec((COMBINE_T, D_MODEL), lambda i: (i, 0)),
            pl.BlockSpec((COMBINE_T, LANES), lambda i: (i, 0)),
            pl.BlockSpec((1, D_MODEL), lambda i: (0, 0)),
            pl.BlockSpec(memory_space=pl.ANY),
        ],
        out_specs=pl.BlockSpec((COMBINE_T, D_MODEL), lambda i: (i, 0)),
        out_shape=jax.ShapeDtypeStruct((n, D_MODEL), F32),
        scratch_shapes=[
            pltpu.VMEM((2, COMBINE_T * N_COL, LANES), F32),
            pltpu.VMEM((2, COMBINE_T * N_COL, LANES), F32),
            pltpu.SemaphoreType.DMA((2,)),
        ],
        compiler_params=pltpu.CompilerParams(
            dimension_semantics=("arbitrary",), vmem_limit_bytes=VMEM_LIMIT),
        name="combine",
    )(pos_flat, pos_flat, x, meta, g, y)


def _slot_plan(meta, counts, n_tiles):
    cnt = counts[0, ROUTER_COL0:ROUTER_COL0 + N_EXPERTS].astype(jnp.int32)
    padded = (cnt + EXPERT_TM - 1) // EXPERT_TM * EXPERT_TM
    ends = jnp.cumsum(padded)
    offs = ends - padded
    e = meta[:, 0:2].astype(jnp.int32)
    rank = meta[:, 2:4].astype(jnp.int32)
    onehot = e[..., None] == jnp.arange(N_EXPERTS, dtype=jnp.int32)
    pos = jnp.sum(jnp.where(onehot, offs, 0), axis=-1) + rank
    tile_ends = ends // EXPERT_TM
    tile_ids = jnp.arange(n_tiles, dtype=jnp.int32)
    tile_expert = jnp.sum(tile_ids[:, None] >= tile_ends[None, :], axis=1).astype(jnp.int32)
    tile_expert = jnp.minimum(tile_expert, N_EXPERTS - 1)
    n_used = tile_ends[-1:].astype(jnp.int32)
    return pos.reshape(-1), tile_expert, n_used, offs + cnt, padded - cnt


def _col_major(w):
    return w.reshape(w.shape[0], N_COL, LANES).transpose(1, 0, 2)


def kernel(x, mem, mix_norm_g, mem_norm_g, w_in, b_gate, conv_a_w, w_out_a, b_glu, conv_b_w,
           conv_b_b, ln_b_g, ln_b_b, w_out_b, b_out_b, w_kv, w_out_x, w_o, ffn_norm_g, w_group,
           b_group, w_router, b_router, w_gate_up, w_down, final_norm_g):
    batch, seq, d = x.shape
    depth = w_in.shape[0]
    n = batch * seq
    n_tiles = (2 * n) // EXPERT_TM + N_EXPERTS
    n_slots = n_tiles * EXPERT_TM
    row = lambda a: a.reshape(1, -1)
    xf = x.reshape(n, d)
    pad = LANES - N_GROUPS - N_EXPERTS
    for l in range(depth):
        k, v = _kv_proj(mem, row(mem_norm_g[l]), w_kv[l].astype(BF16))
        h = _proj_in(xf, row(mix_norm_g[l]), w_in, l)
        w_r = jnp.pad(jnp.concatenate([w_group[l], w_router[l]], axis=1), ((0, 0), (0, pad)))
        b_r = jnp.pad(jnp.concatenate([b_group[l], b_router[l]]), (0, pad)).reshape(1, LANES)
        x1, z, meta, counts = _mixer(
            h, xf, k, v, _col_major(conv_a_w[l]), _col_major(conv_b_w[l]), row(b_glu[l]),
            _col_major(row(conv_b_b[l])), row(ln_b_g[l]), row(ln_b_b[l]), row(b_out_b[l]),
            row(b_gate[l]), w_out_a[l].astype(BF16), w_out_b[l].astype(BF16),
            w_out_x[l].astype(BF16), w_o[l].astype(BF16), row(ffn_norm_g[l]), w_r, b_r, batch)
        pos, tile_expert, n_used, pad_start, pad_len = _slot_plan(meta, counts, n_tiles)
        zs = _dispatch(pos, pad_start, pad_len, n_used, z, n_slots)
        y = _experts(tile_expert, n_used, zs, w_gate_up, w_down, l)
        xf = _combine(pos, x1, meta, row(final_norm_g), y, final_norm=(l == depth - 1))
    return xf.reshape(batch, seq, d)
```
